```python
import jax, jax.numpy as jnp
from jax import lax
import numpy as np

D_MODEL = 4096
BATCH = 2
SEQ = 8192
DEPTH = 1

PLE_DIM = 256
POOL_WIDTH = D_MODEL // 2
POOL_WINDOWS = (2, 4, 8, 16)
POOL_GROUPS = len(POOL_WINDOWS)
POOL_GROUP_W = POOL_WIDTH // POOL_GROUPS
HEAD_DIM = 128
ATTN_WIDTH = D_MODEL // 2
N_HEADS = ATTN_WIDTH // HEAD_DIM
MOBA_BLOCK = 256
MOBA_TOPK = 3
Q_CHUNK = 32
RMS_EPS = 1e-6
IN_WIDTH = 2 * POOL_WIDTH + 4 * ATTN_WIDTH + 2 * D_MODEL

kernel_name = "hybrid_pool_moba_gated_merge"


def rms_norm(x, gain):
    xf = x.astype(jnp.float32)
    y = xf * lax.rsqrt(jnp.mean(xf * xf, axis=-1, keepdims=True) + RMS_EPS)
    return (y * gain.astype(jnp.float32)).astype(x.dtype)


def multiscale_pool(u, group_w, scale):
    B_, S_, _ = u.shape
    uf = u.astype(jnp.float32)
    c = lax.cumsum(uf, axis=1)
    wmax = max(POOL_WINDOWS)
    cpad = jnp.pad(c, ((0, 0), (wmax, 0), (0, 0)))
    t = jnp.arange(S_)[None, :, None]
    outs = []
    for g, w in enumerate(POOL_WINDOWS):
        lo, hi = g * POOL_GROUP_W, (g + 1) * POOL_GROUP_W
        prev = cpad[:, wmax - w: wmax - w + S_, lo:hi]
        count = jnp.minimum(t + 1, w).astype(jnp.float32)
        outs.append((c[..., lo:hi] - prev) / count)
    pooled = jnp.stack(outs, axis=2)
    diff = (pooled - uf.reshape(B_, S_, POOL_GROUPS, POOL_GROUP_W)).astype(u.dtype)
    mixed = jnp.einsum('bsgc,gcd->bsgd', diff, group_w).reshape(B_, S_, POOL_WIDTH)
    return mixed * scale


def moba_attention(q, k, v):
    B_, S_, H_, Dh = q.shape
    f32 = jnp.float32
    nb = -(-S_ // MOBA_BLOCK)
    s_pad = nb * MOBA_BLOCK
    pad = ((0, 0), (0, s_pad - S_), (0, 0), (0, 0))
    qf = jnp.pad(q, pad).astype(f32) * (Dh ** -0.5)
    kb = jnp.pad(k, pad).astype(f32).reshape(B_, nb, MOBA_BLOCK, H_, Dh).transpose(0, 3, 1, 2, 4)
    vb = jnp.pad(v, pad).astype(f32).reshape(B_, nb, MOBA_BLOCK, H_, Dh).transpose(0, 3, 1, 2, 4)
    kmean = jnp.mean(kb, axis=3)
    n_chunks = s_pad // Q_CHUNK
    qc = qf.reshape(B_, n_chunks, Q_CHUNK, H_, Dh).transpose(1, 0, 3, 2, 4)
    n_sel = min(MOBA_TOPK, nb)
    blk_ids = jnp.arange(nb)
    gather_blocks = jax.vmap(jax.vmap(lambda blocks, idx: blocks[idx]))

    def one_chunk(args):
        q_c, ci = args
        q_pos = ci * Q_CHUNK + jnp.arange(Q_CHUNK)
        j = (ci * Q_CHUNK) // MOBA_BLOCK
        gate = jnp.einsum('bhqd,bhnd->bhqn', q_c, kmean)
        gate = jnp.where(blk_ids < j, gate, -jnp.inf)
        _, sel = lax.top_k(gate, n_sel)
        sel_ok = sel < j
        k_sel = gather_blocks(kb, sel)
        v_sel = gather_blocks(vb, sel)
        s_sel = jnp.einsum('bhqd,bhqnmd->bhqnm', q_c, k_sel)
        s_sel = jnp.where(sel_ok[..., None], s_sel, -jnp.inf)
        k_own = lax.dynamic_index_in_dim(kb, j, axis=2, keepdims=False)
        v_own = lax.dynamic_index_in_dim(vb, j, axis=2, keepdims=False)
        s_own = jnp.einsum('bhqd,bhmd->bhqm', q_c, k_own)
        k_pos = j * MOBA_BLOCK + jnp.arange(MOBA_BLOCK)
        s_own = jnp.where(k_pos[None, :] <= q_pos[:, None], s_own, -jnp.inf)
        s_all = jnp.concatenate(
            [s_sel.reshape(B_, H_, Q_CHUNK, n_sel * MOBA_BLOCK), s_own], axis=-1)
        prob = jax.nn.softmax(s_all, axis=-1)
        p_sel = prob[..., :n_sel * MOBA_BLOCK].reshape(B_, H_, Q_CHUNK, n_sel, MOBA_BLOCK)
        p_own = prob[..., n_sel * MOBA_BLOCK:]
        return (jnp.einsum('bhqnm,bhqnmd->bhqd', p_sel, v_sel)
                + jnp.einsum('bhqm,bhmd->bhqd', p_own, v_own))

    out = lax.map(one_chunk, (qc, jnp.arange(n_chunks)))
    out = out.transpose(1, 0, 3, 2, 4).reshape(B_, s_pad, H_ * Dh)[:, :S_]
    return out.astype(q.dtype)


def setup_inputs(seed: int = 0) -> dict:
    key = jax.random.key(seed)
    ks = jax.random.split(key, 14)
    f32 = jnp.float32
    nrm = lambda k, shape, fan_in: jax.random.normal(k, shape, f32) * (fan_in ** -0.5)
    return {
        "x": jax.random.normal(ks[0], (BATCH, SEQ, D_MODEL), f32),
        "p": jax.random.normal(ks[1], (DEPTH, BATCH, SEQ, PLE_DIM), f32),
        "norm_pre": 1.0 + 0.02 * jax.random.normal(ks[2], (DEPTH, D_MODEL), f32),
        "w_in": nrm(ks[3], (DEPTH, D_MODEL, IN_WIDTH), D_MODEL),
        "pool_group_w": nrm(ks[4], (DEPTH, POOL_GROUPS, POOL_GROUP_W, POOL_GROUP_W), POOL_GROUP_W),
        "pool_scale": 1.0 + 0.1 * jax.random.normal(ks[5], (DEPTH, POOL_WIDTH), f32),
        "w_pool_out": nrm(ks[6], (DEPTH, POOL_WIDTH, D_MODEL), POOL_WIDTH),
        "w_attn_out": nrm(ks[7], (DEPTH, ATTN_WIDTH, D_MODEL), ATTN_WIDTH),
        "w_out": nrm(ks[8], (DEPTH, D_MODEL, D_MODEL), D_MODEL),
        "norm_post": 1.0 + 0.02 * jax.random.normal(ks[9], (DEPTH, D_MODEL), f32),
        "w_ple_proj": nrm(ks[10], (DEPTH, PLE_DIM, D_MODEL), PLE_DIM),
        "w_ple_gate": nrm(ks[11], (DEPTH, D_MODEL, D_MODEL), D_MODEL),
    }


def reference(x, p, norm_pre, w_in, pool_group_w, pool_scale, w_pool_out, w_attn_out,
              w_out, norm_post, w_ple_proj, w_ple_gate):
    B_, S_, _ = x.shape
    cuts = [POOL_WIDTH, POOL_WIDTH, ATTN_WIDTH, ATTN_WIDTH, ATTN_WIDTH, ATTN_WIDTH, D_MODEL]
    split_points = [int(c) for c in np.cumsum(cuts)]
    for i in range(DEPTH):
        h = rms_norm(x, norm_pre[i])
        z = h @ w_in[i]
        u_pool, g_pool, q, k, v, g_attn, m_pool, m_attn = jnp.split(z, split_points, axis=-1)
        y_pool = multiscale_pool(u_pool, pool_group_w[i], pool_scale[i]) * jax.nn.silu(g_pool)
        y_attn = moba_attention(q.reshape(B_, S_, N_HEADS, HEAD_DIM),
                                k.reshape(B_, S_, N_HEADS, HEAD_DIM),
                                v.reshape(B_, S_, N_HEADS, HEAD_DIM)) * jax.nn.silu(g_attn)
        merged = (jax.nn.sigmoid(m_pool) * (y_pool @ w_pool_out[i])
                  + jax.nn.sigmoid(m_attn) * (y_attn @ w_attn_out[i]))
        x = x + rms_norm(merged @ w_out[i], norm_post[i])
        x = x + jax.nn.sigmoid(x @ w_ple_gate[i]) * (p[i] @ w_ple_proj[i])
    return x
```

```python
import functools

import jax
import jax.numpy as jnp
from jax import lax
from jax.experimental import pallas as pl
from jax.experimental.pallas import tpu as pltpu

F32 = jnp.float32
BF16 = jnp.bfloat16

D_MODEL = 4096
PLE_DIM = 256
POOL_WIDTH = D_MODEL // 2
POOL_WINDOWS = (2, 4, 8, 16)
POOL_GROUP_W = POOL_WIDTH // len(POOL_WINDOWS)
POOL_HALO = 16
HEAD_DIM = 128
ATTN_WIDTH = D_MODEL // 2
N_HEADS = ATTN_WIDTH // HEAD_DIM
MOBA_BLOCK = 256
MOBA_TOPK = 3
RMS_EPS = 1e-6
IN_WIDTH = 2 * POOL_WIDTH + 4 * ATTN_WIDTH + 2 * D_MODEL

COL_U = 0
COL_GP = COL_U + POOL_WIDTH
COL_Q = COL_GP + POOL_WIDTH
COL_K = COL_Q + ATTN_WIDTH
COL_V = COL_K + ATTN_WIDTH
COL_GA = COL_V + ATTN_WIDTH
COL_MP = COL_GA + ATTN_WIDTH
COL_MA = COL_MP + D_MODEL

KEY_CHUNK_BLOCKS = 2
KEY_CHUNK = KEY_CHUNK_BLOCKS * MOBA_BLOCK
MASK_NEG = -1e30

V7X_VMEM_LIMIT_BYTES = 56 * 1024 * 1024


def _params(*semantics):
    return pltpu.CompilerParams(dimension_semantics=semantics,
                                vmem_limit_bytes=V7X_VMEM_LIMIT_BYTES)


def _sigmoid(v):
    return 1.0 / (1.0 + jnp.exp(-v))


def _rmsnorm_kernel(x_ref, g_ref, o_ref):
    x = x_ref[...]
    ms = jnp.mean(x * x, axis=-1, keepdims=True)
    o_ref[...] = (x * lax.rsqrt(ms + RMS_EPS) * g_ref[...]).astype(o_ref.dtype)


def _rmsnorm(x2d, gain, tm=256):
    t, d = x2d.shape
    return pl.pallas_call(
        _rmsnorm_kernel,
        grid=(t // tm,),
        in_specs=[pl.BlockSpec((tm, d), lambda i: (i, 0)),
                  pl.BlockSpec((1, d), lambda i: (0, 0))],
        out_specs=pl.BlockSpec((tm, d), lambda i: (i, 0)),
        out_shape=jax.ShapeDtypeStruct((t, d), BF16),
        compiler_params=_params("arbitrary"),
        name="rmsnorm_pre",
    )(x2d, gain)


def _win_kernel(h_ref, w_ref, o_ref, *, tn):
    col0 = pl.program_id(1) * tn
    acc = jnp.dot(h_ref[...], w_ref[...], preferred_element_type=F32)

    is_silu = ((col0 >= COL_GP) & (col0 < COL_Q)) | ((col0 >= COL_GA) & (col0 < COL_MP))
    is_q = (col0 >= COL_Q) & (col0 < COL_K)
    is_sig = col0 >= COL_MP
    is_plain = jnp.logical_not(is_silu | is_q | is_sig)

    @pl.when(is_plain)
    def _():
        o_ref[...] = acc.astype(o_ref.dtype)

    @pl.when(is_q)
    def _():
        o_ref[...] = (acc * (HEAD_DIM ** -0.5)).astype(o_ref.dtype)

    @pl.when(is_silu)
    def _():
        o_ref[...] = (acc * _sigmoid(acc)).astype(o_ref.dtype)

    @pl.when(is_sig)
    def _():
        o_ref[...] = _sigmoid(acc).astype(o_ref.dtype)


def _in_proj(h, w_in, tm=1024, tn=1024):
    t, k = h.shape
    n = w_in.shape[1]
    return pl.pallas_call(
        functools.partial(_win_kernel, tn=tn),
        grid=(t // tm, n // tn),
        in_specs=[pl.BlockSpec((tm, k), lambda i, j: (i, 0)),
                  pl.BlockSpec((k, tn), lambda i, j: (0, j))],
        out_specs=pl.BlockSpec((tm, tn), lambda i, j: (i, j)),
        out_shape=jax.ShapeDtypeStruct((t, n), BF16),
        compiler_params=_params("arbitrary", "arbitrary"),
        name="in_proj",
    )(h, w_in)


def _pool_kernel(u_ref, sg_ref, w_ref, sc_ref, o_ref, hist_ref, *, tm):
    ti = pl.program_id(1)

    @pl.when(ti == 0)
    def _():
        hist_ref[0:POOL_HALO, :] = jnp.zeros((POOL_HALO, POOL_WIDTH), F32)

    @pl.when(ti > 0)
    def _():
        hist_ref[0:POOL_HALO, :] = hist_ref[tm:tm + POOL_HALO, :]

    hist_ref[POOL_HALO:POOL_HALO + tm, :] = u_ref[...].astype(F32)
    pos = ti * tm + lax.broadcasted_iota(jnp.int32, (tm, 1), 0)
    for g, w in enumerate(POOL_WINDOWS):
        c0 = g * POOL_GROUP_W
        cols = pl.ds(c0, POOL_GROUP_W)
        u = hist_ref[pl.ds(POOL_HALO, tm), cols]
        acc = u
        for d in range(1, w):
            acc = acc + hist_ref[pl.ds(POOL_HALO - d, tm), cols]
        count = jnp.minimum(pos + 1, w).astype(F32)
        diff = acc / count - u
        mixed = jnp.dot(diff.astype(BF16), w_ref[g], preferred_element_type=F32)
        y = mixed * sc_ref[:, cols] * sg_ref[:, cols].astype(F32)
        o_ref[:, cols] = y.astype(o_ref.dtype)


def _pool_branch(z, group_w, scale, batch, seq, tm=512):
    nt = seq // tm
    ublk = COL_U // POOL_WIDTH
    gblk = COL_GP // POOL_WIDTH
    return pl.pallas_call(
        functools.partial(_pool_kernel, tm=tm),
        grid=(batch, nt),
        in_specs=[pl.BlockSpec((tm, POOL_WIDTH), lambda b, i: (b * nt + i, ublk)),
                  pl.BlockSpec((tm, POOL_WIDTH), lambda b, i: (b * nt + i, gblk)),
                  pl.BlockSpec(group_w.shape, lambda b, i: (0, 0, 0)),
                  pl.BlockSpec((1, POOL_WIDTH), lambda b, i: (0, 0))],
        out_specs=pl.BlockSpec((tm, POOL_WIDTH), lambda b, i: (b * nt + i, 0)),
        out_shape=jax.ShapeDtypeStruct((batch * seq, POOL_WIDTH), BF16),
        scratch_shapes=[pltpu.VMEM((POOL_HALO + tm, POOL_WIDTH), F32)],
        compiler_params=_params("arbitrary", "arbitrary"),
        name="pool_branch",
    )(z, z, group_w, scale)


def _attn_kernel(q_ref, k_ref, v_ref, sg_ref, o_ref, kext_ref, vt_ref, kmean_ref, qext_ref, *, n_blocks):
    i = pl.program_id(2)
    n_chunks = n_blocks // KEY_CHUNK_BLOCKS

    @pl.when(i == 0)
    def _prepare_keys_values():
        qext_ref[...] = jnp.zeros(qext_ref.shape, BF16)
        lane = lax.broadcasted_iota(jnp.int32, (MOBA_BLOCK, HEAD_DIM), 1)

        def chunk_body(c, carry):
            for s in range(KEY_CHUNK_BLOCKS):
                blk = c * KEY_CHUNK_BLOCKS + s
                r0 = pl.multiple_of(blk * MOBA_BLOCK, MOBA_BLOCK)
                kb = k_ref[pl.ds(r0, MOBA_BLOCK), :]
                rows = pl.ds(s * MOBA_BLOCK, MOBA_BLOCK)
                kext_ref[c, rows, 0:HEAD_DIM] = kb
                kext_ref[c, rows, HEAD_DIM:2 * HEAD_DIM] = jnp.where(lane == blk, 1.0, 0.0).astype(BF16)
                kmean_ref[pl.ds(blk, 1), :] = jnp.mean(kb.astype(F32), axis=0, keepdims=True)
                vb = v_ref[pl.ds(r0, MOBA_BLOCK), :].astype(F32)
                vt_ref[c, :, s * MOBA_BLOCK:(s + 1) * MOBA_BLOCK] = vb.T.astype(BF16)
            return carry

        lax.fori_loop(0, n_chunks, chunk_body, 0)

    q_t = q_ref[...].astype(F32).T.astype(BF16)
    gate = jnp.dot(kmean_ref[...].astype(BF16), q_t, preferred_element_type=F32)
    blk_id = lax.broadcasted_iota(jnp.int32, gate.shape, 0)
    is_past = blk_id < i
    past_f = jnp.where(is_past, 1.0, 0.0)
    g = jnp.where(is_past, gate, -jnp.inf)
    sel = jnp.where(blk_id == i, 1.0, 0.0)
    for _ in range(MOBA_TOPK):
        mx = jnp.max(g, axis=0, keepdims=True)
        first = jnp.min(jnp.where(g == mx, blk_id, n_blocks), axis=0, keepdims=True)
        pick = blk_id == first
        sel = jnp.where(pick, jnp.maximum(sel, past_f), sel)
        g = jnp.where(pick, -jnp.inf, g)
    qext_ref[0:HEAD_DIM, :] = q_t
    qext_ref[HEAD_DIM:HEAD_DIM + n_blocks, :] = jnp.where(sel > 0.5, 0.0, MASK_NEG).astype(BF16)

    def chunk_scores(c):
        return jnp.dot(kext_ref[c], qext_ref[...], preferred_element_type=F32)

    def online_update(c, s_t, carry):
        m, l, acc = carry
        m_new = jnp.maximum(m, jnp.max(s_t, axis=0, keepdims=True))
        alpha = jnp.exp(m - m_new)
        p_t = jnp.exp(s_t - m_new)
        l = alpha * l + jnp.sum(p_t, axis=0, keepdims=True)
        acc = alpha * acc + jnp.dot(vt_ref[c], p_t.astype(BF16), preferred_element_type=F32)
        return m_new, l, acc

    def past_body(c, carry):
        return online_update(c, chunk_scores(c), carry)

    c_own = i // KEY_CHUNK_BLOCKS
    carry = (jnp.full((1, MOBA_BLOCK), MASK_NEG, F32),
             jnp.zeros((1, MOBA_BLOCK), F32),
             jnp.zeros((HEAD_DIM, MOBA_BLOCK), F32))
    carry = lax.fori_loop(0, c_own, past_body, carry)

    s_t = chunk_scores(c_own)
    key_pos = c_own * KEY_CHUNK + lax.broadcasted_iota(jnp.int32, s_t.shape, 0)
    q_pos = i * MOBA_BLOCK + lax.broadcasted_iota(jnp.int32, s_t.shape, 1)
    s_t = jnp.where(key_pos > q_pos, MASK_NEG, s_t)
    _, l, acc = online_update(c_own, s_t, carry)

    out = (acc * (1.0 / l)).T * sg_ref[...].astype(F32)
    o_ref[...] = out.astype(o_ref.dtype)


def _attn_branch(z, batch, seq):
    nb = seq // MOBA_BLOCK
    n_chunks = nb // KEY_CHUNK_BLOCKS
    qblk, kblk, vblk, gblk = (c // HEAD_DIM for c in (COL_Q, COL_K, COL_V, COL_GA))
    return pl.pallas_call(
        functools.partial(_attn_kernel, n_blocks=nb),
        grid=(batch, N_HEADS, nb),
        in_specs=[pl.BlockSpec((MOBA_BLOCK, HEAD_DIM), lambda b, h, i: (b * nb + i, qblk + h)),
                  pl.BlockSpec((seq, HEAD_DIM), lambda b, h, i: (b, kblk + h)),
                  pl.BlockSpec((seq, HEAD_DIM), lambda b, h, i: (b, vblk + h)),
                  pl.BlockSpec((MOBA_BLOCK, HEAD_DIM), lambda b, h, i: (b * nb + i, gblk + h))],
        out_specs=pl.BlockSpec((MOBA_BLOCK, HEAD_DIM), lambda b, h, i: (b * nb + i, h)),
        out_shape=jax.ShapeDtypeStruct((batch * seq, ATTN_WIDTH), BF16),
        scratch_shapes=[pltpu.VMEM((n_chunks, KEY_CHUNK, 2 * HEAD_DIM), BF16),
                        pltpu.VMEM((n_chunks, HEAD_DIM, KEY_CHUNK), BF16),
                        pltpu.VMEM((nb, HEAD_DIM), F32),
                        pltpu.VMEM((2 * HEAD_DIM, MOBA_BLOCK), BF16)],
        compiler_params=_params("arbitrary", "arbitrary", "arbitrary"),
        name="moba_attention",
    )(z, z, z, z)


def _merge_kernel(yp_ref, ya_ref, wp_ref, wa_ref, mp_ref, ma_ref, o_ref):
    pool = jnp.dot(yp_ref[...], wp_ref[...], preferred_element_type=F32)
    attn = jnp.dot(ya_ref[...], wa_ref[...], preferred_element_type=F32)
    merged = mp_ref[...].astype(F32) * pool + ma_ref[...].astype(F32) * attn
    o_ref[...] = merged.astype(o_ref.dtype)


def _merge(y_pool, y_attn, w_pool_out, w_attn_out, z, tm=1024, tn=1024):
    t = y_pool.shape[0]
    mpblk, mablk = COL_MP // tn, COL_MA // tn
    return pl.pallas_call(
        _merge_kernel,
        grid=(t // tm, D_MODEL // tn),
        in_specs=[pl.BlockSpec((tm, POOL_WIDTH), lambda i, j: (i, 0)),
                  pl.BlockSpec((tm, ATTN_WIDTH), lambda i, j: (i, 0)),
                  pl.BlockSpec((POOL_WIDTH, tn), lambda i, j: (0, j)),
                  pl.BlockSpec((ATTN_WIDTH, tn), lambda i, j: (0, j)),
                  pl.BlockSpec((tm, tn), lambda i, j: (i, mpblk + j)),
                  pl.BlockSpec((tm, tn), lambda i, j: (i, mablk + j))],
        out_specs=pl.BlockSpec((tm, tn), lambda i, j: (i, j)),
        out_shape=jax.ShapeDtypeStruct((t, D_MODEL), BF16),
        compiler_params=_params("arbitrary", "arbitrary"),
        name="gated_merge",
    )(y_pool, y_attn, w_pool_out, w_attn_out, z, z)


def _matmul_kernel(a_ref, b_ref, o_ref):
    o_ref[...] = jnp.dot(a_ref[...], b_ref[...], preferred_element_type=F32).astype(o_ref.dtype)


def _out_proj(a, w, tm=1024, tn=1024):
    t, k = a.shape
    n = w.shape[1]
    return pl.pallas_call(
        _matmul_kernel,
        grid=(t // tm, n // tn),
        in_specs=[pl.BlockSpec((tm, k), lambda i, j: (i, 0)),
                  pl.BlockSpec((k, tn), lambda i, j: (0, j))],
        out_specs=pl.BlockSpec((tm, tn), lambda i, j: (i, j)),
        out_shape=jax.ShapeDtypeStruct((t, n), BF16),
        compiler_params=_params("arbitrary", "arbitrary"),
        name="out_proj",
    )(a, w)


def _ple_kernel(x_ref, t_ref, g_ref, wg_ref, p_ref, wp_ref, o_ref, x1b_ref, rstd_ref, *, tn):
    j = pl.program_id(1)

    @pl.when(j == 0)
    def _():
        t = t_ref[...].astype(F32)
        rstd = lax.rsqrt(jnp.mean(t * t, axis=-1, keepdims=True) + RMS_EPS)
        rstd_ref[...] = rstd
        x1b_ref[...] = (x_ref[...] + t * rstd * g_ref[...]).astype(BF16)

    cols = pl.ds(pl.multiple_of(j * tn, tn), tn)
    x1 = x_ref[:, cols] + t_ref[:, cols].astype(F32) * rstd_ref[...] * g_ref[:, cols]
    gate = jnp.dot(x1b_ref[...], wg_ref[...], preferred_element_type=F32)
    emb = jnp.dot(p_ref[...].astype(BF16), wp_ref[...], preferred_element_type=F32)
    o_ref[...] = x1 + _sigmoid(gate) * emb


def _ple(x2d, t_proj, gain, w_gate, p2d, w_proj, tm=512, tn=512):
    t, d = x2d.shape
    return pl.pallas_call(
        functools.partial(_ple_kernel, tn=tn),
        grid=(t // tm, d // tn),
        in_specs=[pl.BlockSpec((tm, d), lambda i, j: (i, 0)),
                  pl.BlockSpec((tm, d), lambda i, j: (i, 0)),
                  pl.BlockSpec((1, d), lambda i, j: (0, 0)),
                  pl.BlockSpec((d, tn), lambda i, j: (0, j)),
                  pl.BlockSpec((tm, PLE_DIM), lambda i, j: (i, 0)),
                  pl.BlockSpec((PLE_DIM, tn), lambda i, j: (0, j))],
        out_specs=pl.BlockSpec((tm, tn), lambda i, j: (i, j)),
        out_shape=jax.ShapeDtypeStruct((t, d), F32),
        scratch_shapes=[pltpu.VMEM((tm, d), BF16),
                        pltpu.VMEM((tm, 1), F32)],
        compiler_params=_params("arbitrary", "arbitrary"),
        name="ple_residual",
    )(x2d, t_proj, gain, w_gate, p2d, w_proj)


def kernel(x, p, norm_pre, w_in, pool_group_w, pool_scale, w_pool_out, w_attn_out, w_out, norm_post, w_ple_proj, w_ple_gate):
    batch, seq, d = x.shape
    depth = w_in.shape[0]
    assert d == D_MODEL and seq % KEY_CHUNK == 0 and w_in.shape[2] == IN_WIDTH
    x2d = x.reshape(batch * seq, d)
    for layer in range(depth):
        h = _rmsnorm(x2d, norm_pre[layer].reshape(1, d))
        z = _in_proj(h, w_in[layer].astype(BF16))
        y_pool = _pool_branch(z, pool_group_w[layer].astype(BF16), pool_scale[layer].reshape(1, POOL_WIDTH), batch, seq)
        y_attn = _attn_branch(z, batch, seq)
        merged = _merge(y_pool, y_attn, w_pool_out[layer].astype(BF16), w_attn_out[layer].astype(BF16), z)
        t_proj = _out_proj(merged, w_out[layer].astype(BF16))
        x2d = _ple(x2d, t_proj, norm_post[layer].reshape(1, d), w_ple_gate[layer].astype(BF16),
                   p[layer].reshape(batch * seq, PLE_DIM), w_ple_proj[layer].astype(BF16))
    return x2d.reshape(batch, seq, d)
```

```python
import functools

import jax
import jax.numpy as jnp
from jax import lax
from jax.experimental import pallas as pl
from jax.experimental.pallas import tpu as pltpu

F32 = jnp.float32
BF16 = jnp.bfloat16

D_MODEL = 4096
PLE_DIM = 256
POOL_WIDTH = D_MODEL // 2
POOL_WINDOWS = (2, 4, 8, 16)
POOL_GROUP_W = POOL_WIDTH // len(POOL_WINDOWS)
POOL_HALO = 16
HEAD_DIM = 128
ATTN_WIDTH = D_MODEL // 2
N_HEADS = ATTN_WIDTH // HEAD_DIM
MOBA_BLOCK = 256
MOBA_TOPK = 3
RMS_EPS = 1e-6
IN_WIDTH = 2 * POOL_WIDTH + 4 * ATTN_WIDTH + 2 * D_MODEL

COL_U = 0
COL_GP = COL_U + POOL_WIDTH
COL_Q = COL_GP + POOL_WIDTH
COL_K = COL_Q + ATTN_WIDTH
COL_V = COL_K + ATTN_WIDTH
COL_GA = COL_V + ATTN_WIDTH
COL_MP = COL_GA + ATTN_WIDTH
COL_MA = COL_MP + D_MODEL

KEY_CHUNK_BLOCKS = 2
KEY_CHUNK = KEY_CHUNK_BLOCKS * MOBA_BLOCK
HEADS_PER_STEP = 4
VT_ROWS = HEAD_DIM + 16
LOG2_E = 1.4426950408889634
Q_SCALE = HEAD_DIM ** -0.5 * LOG2_E
MASK_NEG = -1e30

V7X_VMEM_LIMIT_BYTES = 56 * 1024 * 1024


def _params(*semantics):
    return pltpu.CompilerParams(dimension_semantics=semantics,
                                vmem_limit_bytes=V7X_VMEM_LIMIT_BYTES)


def _sigmoid(v):
    return 1.0 / (1.0 + jnp.exp(-v))


def _rmsnorm_kernel(x_ref, g_ref, o_ref):
    x = x_ref[...]
    ms = jnp.mean(x * x, axis=-1, keepdims=True)
    o_ref[...] = (x * lax.rsqrt(ms + RMS_EPS) * g_ref[...]).astype(o_ref.dtype)


def _rmsnorm(x2d, gain, tm=256):
    t, d = x2d.shape
    return pl.pallas_call(
        _rmsnorm_kernel,
        grid=(t // tm,),
        in_specs=[pl.BlockSpec((tm, d), lambda i: (i, 0)),
                  pl.BlockSpec((1, d), lambda i: (0, 0))],
        out_specs=pl.BlockSpec((tm, d), lambda i: (i, 0)),
        out_shape=jax.ShapeDtypeStruct((t, d), BF16),
        compiler_params=_params("arbitrary"),
        name="rmsnorm_pre",
    )(x2d, gain)


def _win_kernel(h_ref, w_ref, o_ref, *, tn):
    col0 = pl.program_id(1) * tn
    acc = jnp.dot(h_ref[...], w_ref[...], preferred_element_type=F32)

    is_silu = ((col0 >= COL_GP) & (col0 < COL_Q)) | ((col0 >= COL_GA) & (col0 < COL_MP))
    is_q = (col0 >= COL_Q) & (col0 < COL_K)
    is_sig = col0 >= COL_MP
    is_plain = jnp.logical_not(is_silu | is_q | is_sig)

    @pl.when(is_plain)
    def _():
        o_ref[...] = acc.astype(o_ref.dtype)

    @pl.when(is_q)
    def _():
        o_ref[...] = (acc * Q_SCALE).astype(o_ref.dtype)

    @pl.when(is_silu)
    def _():
        o_ref[...] = (acc * _sigmoid(acc)).astype(o_ref.dtype)

    @pl.when(is_sig)
    def _():
        o_ref[...] = _sigmoid(acc).astype(o_ref.dtype)


def _in_proj(h, w_in, tm=1024, tn=1024):
    t, k = h.shape
    n = w_in.shape[1]
    return pl.pallas_call(
        functools.partial(_win_kernel, tn=tn),
        grid=(t // tm, n // tn),
        in_specs=[pl.BlockSpec((tm, k), lambda i, j: (i, 0)),
                  pl.BlockSpec((k, tn), lambda i, j: (0, j))],
        out_specs=pl.BlockSpec((tm, tn), lambda i, j: (i, j)),
        out_shape=jax.ShapeDtypeStruct((t, n), BF16),
        compiler_params=_params("arbitrary", "arbitrary"),
        name="in_proj",
    )(h, w_in)


def _pool_kernel(u_ref, sg_ref, w_ref, sc_ref, o_ref, hist_ref, *, tm):
    ti = pl.program_id(1)

    @pl.when(ti == 0)
    def _():
        hist_ref[0:POOL_HALO, :] = jnp.zeros((POOL_HALO, POOL_WIDTH), F32)

    @pl.when(ti > 0)
    def _():
        hist_ref[0:POOL_HALO, :] = hist_ref[tm:tm + POOL_HALO, :]

    hist_ref[POOL_HALO:POOL_HALO + tm, :] = u_ref[...].astype(F32)
    pos = ti * tm + lax.broadcasted_iota(jnp.int32, (tm, 1), 0)
    for g, w in enumerate(POOL_WINDOWS):
        c0 = g * POOL_GROUP_W
        cols = pl.ds(c0, POOL_GROUP_W)
        u = hist_ref[pl.ds(POOL_HALO, tm), cols]
        acc = u
        for d in range(1, w):
            acc = acc + hist_ref[pl.ds(POOL_HALO - d, tm), cols]
        count = jnp.minimum(pos + 1, w).astype(F32)
        diff = acc / count - u
        mixed = jnp.dot(diff.astype(BF16), w_ref[g], preferred_element_type=F32)
        y = mixed * sc_ref[:, cols] * sg_ref[:, cols].astype(F32)
        o_ref[:, cols] = y.astype(o_ref.dtype)


def _pool_branch(z, group_w, scale, batch, seq, tm=512):
    nt = seq // tm
    ublk = COL_U // POOL_WIDTH
    gblk = COL_GP // POOL_WIDTH
    return pl.pallas_call(
        functools.partial(_pool_kernel, tm=tm),
        grid=(batch, nt),
        in_specs=[pl.BlockSpec((tm, POOL_WIDTH), lambda b, i: (b * nt + i, ublk)),
                  pl.BlockSpec((tm, POOL_WIDTH), lambda b, i: (b * nt + i, gblk)),
                  pl.BlockSpec(group_w.shape, lambda b, i: (0, 0, 0)),
                  pl.BlockSpec((1, POOL_WIDTH), lambda b, i: (0, 0))],
        out_specs=pl.BlockSpec((tm, POOL_WIDTH), lambda b, i: (b * nt + i, 0)),
        out_shape=jax.ShapeDtypeStruct((batch * seq, POOL_WIDTH), BF16),
        scratch_shapes=[pltpu.VMEM((POOL_HALO + tm, POOL_WIDTH), F32)],
        compiler_params=_params("arbitrary", "arbitrary"),
        name="pool_branch",
    )(z, z, group_w, scale)


def _attn_kernel(q_ref, k_ref, v_ref, sg_ref, o_ref, onehot_ref, vt_ref, kmean_ref, qext_ref,
                 s_even_ref, s_odd_ref, p_even_ref, p_odd_ref, acc_ref, *, n_blocks):
    i = pl.program_id(2)
    n_chunks = n_blocks // KEY_CHUNK_BLOCKS
    pad_chunk = n_chunks
    heads = range(HEADS_PER_STEP)

    def head_cols(g):
        return pl.ds(g * HEAD_DIM, HEAD_DIM)

    @pl.when(i == 0)
    def _prepare_keys_values():
        qext_ref[...] = jnp.zeros(qext_ref.shape, BF16)
        qext_ref[:, HEAD_DIM + n_blocks:HEAD_DIM + n_blocks + 16, :] = jnp.full(
            (HEADS_PER_STEP, 16, MOBA_BLOCK), MASK_NEG, BF16)
        lane = lax.broadcasted_iota(jnp.int32, (MOBA_BLOCK, HEAD_DIM), 1)
        pad_lane = lax.broadcasted_iota(jnp.int32, (KEY_CHUNK, HEAD_DIM), 1)
        onehot_ref[pad_chunk] = jnp.where(pad_lane == n_blocks, 1.0, 0.0).astype(BF16)
        vt_ref[:, pad_chunk] = jnp.zeros((HEADS_PER_STEP, VT_ROWS, KEY_CHUNK), BF16)
        vt_ref[:, 0:n_chunks, HEAD_DIM:VT_ROWS, :] = jnp.ones(
            (HEADS_PER_STEP, n_chunks, VT_ROWS - HEAD_DIM, KEY_CHUNK), BF16)
        p_even_ref[...] = jnp.zeros(p_even_ref.shape, BF16)

        def chunk_body(c, carry):
            for s in range(KEY_CHUNK_BLOCKS):
                blk = c * KEY_CHUNK_BLOCKS + s
                rows = pl.ds(pl.multiple_of(blk * MOBA_BLOCK, MOBA_BLOCK), MOBA_BLOCK)
                onehot_ref[c, pl.ds(s * MOBA_BLOCK, MOBA_BLOCK), :] = jnp.where(lane == blk, 1.0, 0.0).astype(BF16)
                for g in heads:
                    kb = k_ref[rows, head_cols(g)].astype(F32)
                    kmean_ref[g, pl.ds(blk, 1), :] = jnp.mean(kb, axis=0, keepdims=True)
                    vb = v_ref[rows, head_cols(g)].astype(F32)
                    vt_ref[g, c, 0:HEAD_DIM, s * MOBA_BLOCK:(s + 1) * MOBA_BLOCK] = vb.T.astype(BF16)
            return carry

        lax.fori_loop(0, n_chunks, chunk_body, 0)

    q_ts = [q_ref[:, head_cols(g)].astype(F32).T.astype(BF16) for g in heads]
    gates = [jnp.dot(kmean_ref[g].astype(BF16), q_ts[g], preferred_element_type=F32) for g in heads]
    blk_id = lax.broadcasted_iota(jnp.int32, (n_blocks, MOBA_BLOCK), 0)
    is_past = blk_id < i
    past_f = jnp.where(is_past, 1.0, 0.0)
    for g in heads:
        q_t = q_ts[g]
        gt = jnp.where(is_past, gates[g], -jnp.inf)
        sel = jnp.where(blk_id == i, 1.0, 0.0)
        for _ in range(MOBA_TOPK):
            mx = jnp.max(gt, axis=0, keepdims=True)
            first = jnp.min(jnp.where(gt == mx, blk_id, n_blocks), axis=0, keepdims=True)
            pick = blk_id == first
            sel = jnp.where(pick, jnp.maximum(sel, past_f), sel)
            gt = jnp.where(pick, -jnp.inf, gt)
        qext_ref[g, 0:HEAD_DIM, :] = q_t
        qext_ref[g, HEAD_DIM:HEAD_DIM + n_blocks, :] = jnp.where(sel > 0.5, 0.0, MASK_NEG).astype(BF16)

    def chunk_rows(c):
        return pl.ds(pl.multiple_of(c * KEY_CHUNK, KEY_CHUNK), KEY_CHUNK)

    c_own = i // KEY_CHUNK_BLOCKS
    lead = c_own % 2

    def chunk_at(pos):
        c = pos - lead
        return jnp.where(c < 0, pad_chunk, c)

    def scores_stage(s_ref, pos):
        c = chunk_at(pos)
        rows = chunk_rows(jnp.minimum(c, n_chunks - 1))
        for g in heads:
            k_ext = jnp.concatenate([k_ref[rows, head_cols(g)], onehot_ref[c]], axis=1)
            s_ref[g] = jnp.dot(k_ext, qext_ref[g], preferred_element_type=F32)

    def softmax_stage(s_t, m):
        m_new = jnp.maximum(m, jnp.max(s_t, axis=0, keepdims=True))
        alpha = jnp.exp2(m - m_new)
        p_t = jnp.exp2(s_t - m_new)
        return m_new, alpha, p_t.astype(BF16)

    def value_stage(g, pos, p_ref, alpha):
        acc_ref[g] = alpha * acc_ref[g] + jnp.dot(vt_ref[g, chunk_at(pos)], p_ref[g],
                                                  preferred_element_type=F32)

    def position(pos, s_new_ref, s_old_ref, p_new_ref, p_old_ref, small):
        scores_stage(s_new_ref, pos)
        out = []
        for g in heads:
            alpha_prev, m = small[g]
            m, alpha, p_t = softmax_stage(s_old_ref[g], m)
            p_new_ref[g] = p_t
            value_stage(g, pos - 2, p_old_ref, alpha_prev)
            out.append((alpha, m))
        return out

    def pair_body(u, small):
        pos = 2 * u + 1
        small = position(pos, s_odd_ref, s_even_ref, p_odd_ref, p_even_ref, small)
        small = position(pos + 1, s_even_ref, s_odd_ref, p_even_ref, p_odd_ref, small)
        return tuple(small)

    scores_stage(s_even_ref, 0)
    acc_ref[...] = jnp.zeros(acc_ref.shape, F32)
    small = tuple((jnp.ones((1, MOBA_BLOCK), F32),
                   jnp.full((1, MOBA_BLOCK), MASK_NEG, F32)) for g in heads)
    last = c_own + lead
    small = lax.fori_loop(0, last // 2, pair_body, small)

    key_pos = c_own * KEY_CHUNK + lax.broadcasted_iota(jnp.int32, (KEY_CHUNK, MOBA_BLOCK), 0)
    q_pos = i * MOBA_BLOCK + lax.broadcasted_iota(jnp.int32, (KEY_CHUNK, MOBA_BLOCK), 1)
    is_future = key_pos > q_pos
    for g in heads:
        alpha_prev, m = small[g]
        _, alpha, p_t = softmax_stage(jnp.where(is_future, MASK_NEG, s_even_ref[g]), m)
        p_odd_ref[g] = p_t
        value_stage(g, last - 1, p_even_ref, alpha_prev)
        value_stage(g, last, p_odd_ref, alpha)
        denom = acc_ref[g, HEAD_DIM:HEAD_DIM + 1, :]
        out = (acc_ref[g, 0:HEAD_DIM, :] * (1.0 / denom)).T * sg_ref[:, head_cols(g)].astype(F32)
        o_ref[:, head_cols(g)] = out.astype(o_ref.dtype)


def _attn_branch(z, batch, seq):
    nb = seq // MOBA_BLOCK
    n_chunks = nb // KEY_CHUNK_BLOCKS
    gw = HEADS_PER_STEP * HEAD_DIM
    qblk, kblk, vblk, gblk = (c // gw for c in (COL_Q, COL_K, COL_V, COL_GA))
    return pl.pallas_call(
        functools.partial(_attn_kernel, n_blocks=nb),
        grid=(batch, N_HEADS // HEADS_PER_STEP, nb),
        in_specs=[pl.BlockSpec((MOBA_BLOCK, gw), lambda b, h, i: (b * nb + i, qblk + h)),
                  pl.BlockSpec((seq, gw), lambda b, h, i: (b, kblk + h)),
                  pl.BlockSpec((seq, gw), lambda b, h, i: (b, vblk + h)),
                  pl.BlockSpec((MOBA_BLOCK, gw), lambda b, h, i: (b * nb + i, gblk + h))],
        out_specs=pl.BlockSpec((MOBA_BLOCK, gw), lambda b, h, i: (b * nb + i, h)),
        out_shape=jax.ShapeDtypeStruct((batch * seq, ATTN_WIDTH), BF16),
        scratch_shapes=[pltpu.VMEM((n_chunks + 1, KEY_CHUNK, HEAD_DIM), BF16),
                        pltpu.VMEM((HEADS_PER_STEP, n_chunks + 1, VT_ROWS, KEY_CHUNK), BF16),
                        pltpu.VMEM((HEADS_PER_STEP, nb, HEAD_DIM), F32),
                        pltpu.VMEM((HEADS_PER_STEP, 2 * HEAD_DIM, MOBA_BLOCK), BF16),
                        pltpu.VMEM((HEADS_PER_STEP, KEY_CHUNK, MOBA_BLOCK), F32),
                        pltpu.VMEM((HEADS_PER_STEP, KEY_CHUNK, MOBA_BLOCK), F32),
                        pltpu.VMEM((HEADS_PER_STEP, KEY_CHUNK, MOBA_BLOCK), BF16),
                        pltpu.VMEM((HEADS_PER_STEP, KEY_CHUNK, MOBA_BLOCK), BF16),
                        pltpu.VMEM((HEADS_PER_STEP, VT_ROWS, MOBA_BLOCK), F32)],
        compiler_params=_params("arbitrary", "arbitrary", "arbitrary"),
        name="moba_attention",
    )(z, z, z, z)


def _merge_kernel(yp_ref, ya_ref, wp_ref, wa_ref, mp_ref, ma_ref, o_ref):
    pool = jnp.dot(yp_ref[...], wp_ref[...], preferred_element_type=F32)
    attn = jnp.dot(ya_ref[...], wa_ref[...], preferred_element_type=F32)
    merged = mp_ref[...].astype(F32) * pool + ma_ref[...].astype(F32) * attn
    o_ref[...] = merged.astype(o_ref.dtype)


def _merge(y_pool, y_attn, w_pool_out, w_attn_out, z, tm=1024, tn=1024):
    t = y_pool.shape[0]
    mpblk, mablk = COL_MP // tn, COL_MA // tn
    return pl.pallas_call(
        _merge_kernel,
        grid=(t // tm, D_MODEL // tn),
        in_specs=[pl.BlockSpec((tm, POOL_WIDTH), lambda i, j: (i, 0)),
                  pl.BlockSpec((tm, ATTN_WIDTH), lambda i, j: (i, 0)),
                  pl.BlockSpec((POOL_WIDTH, tn), lambda i, j: (0, j)),
                  pl.BlockSpec((ATTN_WIDTH, tn), lambda i, j: (0, j)),
                  pl.BlockSpec((tm, tn), lambda i, j: (i, mpblk + j)),
                  pl.BlockSpec((tm, tn), lambda i, j: (i, mablk + j))],
        out_specs=pl.BlockSpec((tm, tn), lambda i, j: (i, j)),
        out_shape=jax.ShapeDtypeStruct((t, D_MODEL), BF16),
        compiler_params=_params("arbitrary", "arbitrary"),
        name="gated_merge",
    )(y_pool, y_attn, w_pool_out, w_attn_out, z, z)


def _matmul_kernel(a_ref, b_ref, o_ref):
    o_ref[...] = jnp.dot(a_ref[...], b_ref[...], preferred_element_type=F32).astype(o_ref.dtype)


def _out_proj(a, w, tm=1024, tn=1024):
    t, k = a.shape
    n = w.shape[1]
    return pl.pallas_call(
        _matmul_kernel,
        grid=(t // tm, n // tn),
        in_specs=[pl.BlockSpec((tm, k), lambda i, j: (i, 0)),
                  pl.BlockSpec((k, tn), lambda i, j: (0, j))],
        out_specs=pl.BlockSpec((tm, tn), lambda i, j: (i, j)),
        out_shape=jax.ShapeDtypeStruct((t, n), BF16),
        compiler_params=_params("arbitrary", "arbitrary"),
        name="out_proj",
    )(a, w)


def _ple_kernel(x_ref, t_ref, g_ref, wg_ref, p_ref, wp_ref, o_ref, x1b_ref, rstd_ref, *, tn):
    j = pl.program_id(1)

    @pl.when(j == 0)
    def _():
        t = t_ref[...].astype(F32)
        rstd = lax.rsqrt(jnp.mean(t * t, axis=-1, keepdims=True) + RMS_EPS)
        rstd_ref[...] = rstd
        x1b_ref[...] = (x_ref[...] + t * rstd * g_ref[...]).astype(BF16)

    cols = pl.ds(pl.multiple_of(j * tn, tn), tn)
    x1 = x_ref[:, cols] + t_ref[:, cols].astype(F32) * rstd_ref[...] * g_ref[:, cols]
    gate = jnp.dot(x1b_ref[...], wg_ref[...], preferred_element_type=F32)
    emb = jnp.dot(p_ref[...].astype(BF16), wp_ref[...], preferred_element_type=F32)
    o_ref[...] = x1 + _sigmoid(gate) * emb


def _ple(x2d, t_proj, gain, w_gate, p2d, w_proj, tm=512, tn=512):
    t, d = x2d.shape
    return pl.pallas_call(
        functools.partial(_ple_kernel, tn=tn),
        grid=(t // tm, d // tn),
        in_specs=[pl.BlockSpec((tm, d), lambda i, j: (i, 0)),
                  pl.BlockSpec((tm, d), lambda i, j: (i, 0)),
                  pl.BlockSpec((1, d), lambda i, j: (0, 0)),
                  pl.BlockSpec((d, tn), lambda i, j: (0, j)),
                  pl.BlockSpec((tm, PLE_DIM), lambda i, j: (i, 0)),
                  pl.BlockSpec((PLE_DIM, tn), lambda i, j: (0, j))],
        out_specs=pl.BlockSpec((tm, tn), lambda i, j: (i, j)),
        out_shape=jax.ShapeDtypeStruct((t, d), F32),
        scratch_shapes=[pltpu.VMEM((tm, d), BF16),
                        pltpu.VMEM((tm, 1), F32)],
        compiler_params=_params("arbitrary", "arbitrary"),
        name="ple_residual",
    )(x2d, t_proj, gain, w_gate, p2d, w_proj)


def kernel(x, p, norm_pre, w_in, pool_group_w, pool_scale, w_pool_out, w_attn_out, w_out, norm_post, w_ple_proj, w_ple_gate):
    batch, seq, d = x.shape
    depth = w_in.shape[0]
    assert d == D_MODEL and seq % KEY_CHUNK == 0 and w_in.shape[2] == IN_WIDTH
    x2d = x.reshape(batch * seq, d)
    for layer in range(depth):
        h = _rmsnorm(x2d, norm_pre[layer].reshape(1, d))
        z = _in_proj(h, w_in[layer].astype(BF16))
        y_pool = _pool_branch(z, pool_group_w[layer].astype(BF16), pool_scale[layer].reshape(1, POOL_WIDTH), batch, seq)
        y_attn = _attn_branch(z, batch, seq)
        merged = _merge(y_pool, y_attn, w_pool_out[layer].astype(BF16), w_attn_out[layer].astype(BF16), z)
        t_proj = _out_proj(merged, w_out[layer].astype(BF16))
        x2d = _ple(x2d, t_proj, norm_post[layer].reshape(1, d), w_ple_gate[layer].astype(BF16),
                   p[layer].reshape(batch * seq, PLE_DIM), w_ple_proj[layer].astype(BF16))
    return x2d.reshape(batch, seq, d)
```

```python
import functools

import jax
import jax.numpy as jnp
from jax import lax
from jax.experimental import pallas as pl
from jax.experimental.pallas import tpu as pltpu

F32 = jnp.float32
BF16 = jnp.bfloat16

D_MODEL = 4096
PLE_DIM = 256
POOL_WIDTH = D_MODEL // 2
POOL_WINDOWS = (2, 4, 8, 16)
POOL_GROUP_W = POOL_WIDTH // len(POOL_WINDOWS)
POOL_HALO = 16
HEAD_DIM = 128
ATTN_WIDTH = D_MODEL // 2
N_HEADS = ATTN_WIDTH // HEAD_DIM
MOBA_BLOCK = 256
MOBA_TOPK = 3
RMS_EPS = 1e-6
IN_WIDTH = 2 * POOL_WIDTH + 4 * ATTN_WIDTH + 2 * D_MODEL

COL_U = 0
COL_GP = COL_U + POOL_WIDTH
COL_Q = COL_GP + POOL_WIDTH
COL_K = COL_Q + ATTN_WIDTH
COL_V = COL_K + ATTN_WIDTH
COL_GA = COL_V + ATTN_WIDTH
COL_MP = COL_GA + ATTN_WIDTH
COL_MA = COL_MP + D_MODEL

KEY_CHUNK_BLOCKS = 2
KEY_CHUNK = KEY_CHUNK_BLOCKS * MOBA_BLOCK
HEADS_PER_STEP = 4
VT_ROWS = HEAD_DIM + 16
LOG2_E = 1.4426950408889634
Q_SCALE = HEAD_DIM ** -0.5 * LOG2_E
MASK_NEG = -1e30

V7X_VMEM_LIMIT_BYTES = 56 * 1024 * 1024
MXU_COLS = 256
MXU_ROWS_PER_DOT = 512


def _params(*semantics):
    return pltpu.CompilerParams(dimension_semantics=semantics,
                                vmem_limit_bytes=V7X_VMEM_LIMIT_BYTES)


def _sigmoid(v):
    return 0.5 * jnp.tanh(0.5 * v) + 0.5


def _silu(v):
    h = 0.5 * v
    return h + h * jnp.tanh(h)


def _rmsnorm_kernel(x_ref, g_ref, o_ref):
    x = x_ref[...]
    ms = jnp.mean(x * x, axis=-1, keepdims=True)
    o_ref[...] = (x * lax.rsqrt(ms + RMS_EPS) * g_ref[...]).astype(o_ref.dtype)


def _rmsnorm(x2d, gain, tm=256):
    t, d = x2d.shape
    return pl.pallas_call(
        _rmsnorm_kernel,
        grid=(t // tm,),
        in_specs=[pl.BlockSpec((tm, d), lambda i: (i, 0)),
                  pl.BlockSpec((1, d), lambda i: (0, 0))],
        out_specs=pl.BlockSpec((tm, d), lambda i: (i, 0)),
        out_shape=jax.ShapeDtypeStruct((t, d), BF16),
        compiler_params=_params("arbitrary"),
        name="rmsnorm_pre",
    )(x2d, gain)


def _shadowed_subtiles(tile_shape, matmul, finish):
    tm, tn = tile_shape
    pending = None
    for c0 in range(0, tn, MXU_COLS):
        for r0 in range(0, tm, min(tm, MXU_ROWS_PER_DOT)):
            rows, cols = slice(r0, r0 + min(tm, MXU_ROWS_PER_DOT)), slice(c0, c0 + MXU_COLS)
            acc = matmul(rows, cols)
            if pending is not None:
                finish(*pending)
            pending = (rows, cols, acc)
    finish(*pending)


def _subtiled_matmul(a_ref, b_ref, o_ref, epilogue):
    def matmul(rows, cols):
        return jnp.dot(a_ref[rows, :], b_ref[:, cols], preferred_element_type=F32)

    def finish(rows, cols, acc):
        o_ref[rows, cols] = epilogue(acc).astype(o_ref.dtype)

    _shadowed_subtiles(o_ref.shape, matmul, finish)


def _win_kernel(h_ref, w_ref, o_ref, *, tn):
    col0 = pl.program_id(1) * tn
    is_silu = ((col0 >= COL_GP) & (col0 < COL_Q)) | ((col0 >= COL_GA) & (col0 < COL_MP))
    is_q = (col0 >= COL_Q) & (col0 < COL_K)
    is_sig = col0 >= COL_MP
    is_plain = jnp.logical_not(is_silu | is_q | is_sig)
    kinds = ((is_plain, lambda acc: acc),
             (is_q, lambda acc: acc * Q_SCALE),
             (is_silu, _silu),
             (is_sig, _sigmoid))
    for pred, epilogue in kinds:
        pl.when(pred)(functools.partial(_subtiled_matmul, h_ref, w_ref, o_ref, epilogue))


def _in_proj(h, w_in, tm=1024, tn=1024):
    t, k = h.shape
    n = w_in.shape[1]
    return pl.pallas_call(
        functools.partial(_win_kernel, tn=tn),
        grid=(t // tm, n // tn),
        in_specs=[pl.BlockSpec((tm, k), lambda i, j: (i, 0)),
                  pl.BlockSpec((k, tn), lambda i, j: (0, j))],
        out_specs=pl.BlockSpec((tm, tn), lambda i, j: (i, j)),
        out_shape=jax.ShapeDtypeStruct((t, n), BF16),
        compiler_params=_params("arbitrary", "arbitrary"),
        name="in_proj",
    )(h, w_in)


def _pool_kernel(u_ref, sg_ref, w_ref, sc_ref, o_ref, hist_ref, *, tm):
    ti = pl.program_id(1)

    @pl.when(ti == 0)
    def _():
        hist_ref[0:POOL_HALO, :] = jnp.zeros((POOL_HALO, POOL_WIDTH), F32)

    @pl.when(ti > 0)
    def _():
        hist_ref[0:POOL_HALO, :] = hist_ref[tm:tm + POOL_HALO, :]

    hist_ref[POOL_HALO:POOL_HALO + tm, :] = u_ref[...].astype(F32)
    pos = ti * tm + lax.broadcasted_iota(jnp.int32, (tm, 1), 0)
    for g, w in enumerate(POOL_WINDOWS):
        c0 = g * POOL_GROUP_W
        cols = pl.ds(c0, POOL_GROUP_W)
        u = hist_ref[pl.ds(POOL_HALO, tm), cols]
        acc = u
        for d in range(1, w):
            acc = acc + hist_ref[pl.ds(POOL_HALO - d, tm), cols]
        count = jnp.minimum(pos + 1, w).astype(F32)
        diff = acc / count - u
        mixed = jnp.dot(diff.astype(BF16), w_ref[g], preferred_element_type=F32)
        y = mixed * sc_ref[:, cols] * sg_ref[:, cols].astype(F32)
        o_ref[:, cols] = y.astype(o_ref.dtype)


def _pool_branch(z, group_w, scale, batch, seq, tm=512):
    nt = seq // tm
    ublk = COL_U // POOL_WIDTH
    gblk = COL_GP // POOL_WIDTH
    return pl.pallas_call(
        functools.partial(_pool_kernel, tm=tm),
        grid=(batch, nt),
        in_specs=[pl.BlockSpec((tm, POOL_WIDTH), lambda b, i: (b * nt + i, ublk)),
                  pl.BlockSpec((tm, POOL_WIDTH), lambda b, i: (b * nt + i, gblk)),
                  pl.BlockSpec(group_w.shape, lambda b, i: (0, 0, 0)),
                  pl.BlockSpec((1, POOL_WIDTH), lambda b, i: (0, 0))],
        out_specs=pl.BlockSpec((tm, POOL_WIDTH), lambda b, i: (b * nt + i, 0)),
        out_shape=jax.ShapeDtypeStruct((batch * seq, POOL_WIDTH), BF16),
        scratch_shapes=[pltpu.VMEM((POOL_HALO + tm, POOL_WIDTH), F32)],
        compiler_params=_params("arbitrary", "arbitrary"),
        name="pool_branch",
    )(z, z, group_w, scale)


def _attn_kernel(q_first_ref, q_next_ref, k_ref, v_ref, sg_ref, o_ref, onehot_ref, vt_ref, kmean_ref, qext_ref,
                 s_even_ref, s_odd_ref, p_even_ref, p_odd_ref, acc_ref, *, n_blocks):
    i = pl.program_id(2)
    n_chunks = n_blocks // KEY_CHUNK_BLOCKS
    pad_chunk = n_chunks
    heads = range(HEADS_PER_STEP)

    def head_cols(g):
        return pl.ds(g * HEAD_DIM, HEAD_DIM)

    def chunk_rows(c):
        return pl.ds(pl.multiple_of(c * KEY_CHUNK, KEY_CHUNK), KEY_CHUNK)

    def lead_of(qi):
        return (qi // KEY_CHUNK_BLOCKS) % 2

    def chunk_at(pos, lead):
        c = pos - lead
        return jnp.where(c < 0, pad_chunk, c)

    def scores_stage(s_ref, pos, lead):
        c = chunk_at(pos, lead)
        rows = chunk_rows(jnp.minimum(c, n_chunks - 1))
        for g in heads:
            k_ext = jnp.concatenate([k_ref[rows, head_cols(g)], onehot_ref[c]], axis=1)
            s_ref[g] = jnp.dot(k_ext, qext_ref[g], preferred_element_type=F32)

    def block_gates(src_ref):
        q_ts = [src_ref[:, head_cols(g)].astype(F32).T.astype(BF16) for g in heads]
        gates = [jnp.dot(kmean_ref[g].astype(BF16), q_ts[g], preferred_element_type=F32) for g in heads]
        return q_ts, gates

    def select_blocks(g, q_t, gate, qi):
        blk_id = lax.broadcasted_iota(jnp.int32, (n_blocks, MOBA_BLOCK), 0)
        is_past = blk_id < qi
        past_f = jnp.where(is_past, 1.0, 0.0)
        gt = jnp.where(is_past, gate, -jnp.inf)
        sel = jnp.where(blk_id == qi, 1.0, 0.0)
        for _ in range(MOBA_TOPK):
            mx = jnp.max(gt, axis=0, keepdims=True)
            first = jnp.min(jnp.where(gt == mx, blk_id, n_blocks), axis=0, keepdims=True)
            pick = blk_id == first
            sel = jnp.where(pick, jnp.maximum(sel, past_f), sel)
            gt = jnp.where(pick, -jnp.inf, gt)
        qext_ref[g, 0:HEAD_DIM, :] = q_t
        qext_ref[g, HEAD_DIM:HEAD_DIM + n_blocks, :] = jnp.where(sel > 0.5, 0.0, MASK_NEG).astype(BF16)

    @pl.when(i == 0)
    def _prepare_keys_values():
        qext_ref[...] = jnp.zeros(qext_ref.shape, BF16)
        qext_ref[:, HEAD_DIM + n_blocks:HEAD_DIM + n_blocks + 16, :] = jnp.full(
            (HEADS_PER_STEP, 16, MOBA_BLOCK), MASK_NEG, BF16)
        lane = lax.broadcasted_iota(jnp.int32, (MOBA_BLOCK, HEAD_DIM), 1)
        pad_lane = lax.broadcasted_iota(jnp.int32, (KEY_CHUNK, HEAD_DIM), 1)
        onehot_ref[pad_chunk] = jnp.where(pad_lane == n_blocks, 1.0, 0.0).astype(BF16)
        vt_ref[:, pad_chunk] = jnp.zeros((HEADS_PER_STEP, VT_ROWS, KEY_CHUNK), BF16)
        vt_ref[:, 0:n_chunks, HEAD_DIM:VT_ROWS, :] = jnp.ones(
            (HEADS_PER_STEP, n_chunks, VT_ROWS - HEAD_DIM, KEY_CHUNK), BF16)
        p_even_ref[...] = jnp.zeros(p_even_ref.shape, BF16)

        def chunk_body(c, carry):
            for s in range(KEY_CHUNK_BLOCKS):
                blk = c * KEY_CHUNK_BLOCKS + s
                rows = pl.ds(pl.multiple_of(blk * MOBA_BLOCK, MOBA_BLOCK), MOBA_BLOCK)
                onehot_ref[c, pl.ds(s * MOBA_BLOCK, MOBA_BLOCK), :] = jnp.where(lane == blk, 1.0, 0.0).astype(BF16)
                for g in heads:
                    kb = k_ref[rows, head_cols(g)].astype(F32)
                    kmean_ref[g, pl.ds(blk, 1), :] = jnp.mean(kb, axis=0, keepdims=True)
                    vb = v_ref[rows, head_cols(g)].astype(F32)
                    vt_ref[g, c, 0:HEAD_DIM, s * MOBA_BLOCK:(s + 1) * MOBA_BLOCK] = vb.T.astype(BF16)
            return carry

        lax.fori_loop(0, n_chunks, chunk_body, 0)
        q_ts, gates = block_gates(q_first_ref)
        for g in heads:
            select_blocks(g, q_ts[g], gates[g], i)
        scores_stage(s_even_ref, 0, lead_of(i))

    c_own = i // KEY_CHUNK_BLOCKS
    lead = lead_of(i)

    def softmax_stage(s_t, m):
        m_new = jnp.maximum(m, jnp.max(s_t, axis=0, keepdims=True))
        alpha = jnp.exp2(m - m_new)
        p_t = jnp.exp2(s_t - m_new)
        return m_new, alpha, p_t.astype(BF16)

    def value_stage(g, pos, p_ref, alpha):
        acc_ref[g] = alpha * acc_ref[g] + jnp.dot(vt_ref[g, chunk_at(pos, lead)], p_ref[g],
                                                  preferred_element_type=F32)

    def position(pos, s_new_ref, s_old_ref, p_new_ref, p_old_ref, small):
        scores_stage(s_new_ref, pos, lead)
        out = []
        for g in heads:
            alpha_prev, m = small[g]
            m, alpha, p_t = softmax_stage(s_old_ref[g], m)
            p_new_ref[g] = p_t
            value_stage(g, pos - 2, p_old_ref, alpha_prev)
            out.append((alpha, m))
        return out

    def pair_body(u, small):
        pos = 2 * u + 1
        small = position(pos, s_odd_ref, s_even_ref, p_odd_ref, p_even_ref, small)
        small = position(pos + 1, s_even_ref, s_odd_ref, p_even_ref, p_odd_ref, small)
        return tuple(small)

    acc_ref[...] = jnp.zeros(acc_ref.shape, F32)
    small = tuple((jnp.ones((1, MOBA_BLOCK), F32),
                   jnp.full((1, MOBA_BLOCK), MASK_NEG, F32)) for g in heads)
    last = c_own + lead
    small = lax.fori_loop(0, last // 2, pair_body, small)

    key_pos = c_own * KEY_CHUNK + lax.broadcasted_iota(jnp.int32, (KEY_CHUNK, MOBA_BLOCK), 0)
    q_pos = i * MOBA_BLOCK + lax.broadcasted_iota(jnp.int32, (KEY_CHUNK, MOBA_BLOCK), 1)
    is_future = key_pos > q_pos
    for g in heads:
        alpha_prev, m = small[g]
        _, alpha, p_t = softmax_stage(jnp.where(is_future, MASK_NEG, s_even_ref[g]), m)
        p_odd_ref[g] = p_t
        value_stage(g, last - 1, p_even_ref, alpha_prev)
        value_stage(g, last, p_odd_ref, alpha)
        denom = acc_ref[g, HEAD_DIM:HEAD_DIM + 1, :]
        out = (acc_ref[g, 0:HEAD_DIM, :] * (1.0 / denom)).T * sg_ref[:, head_cols(g)].astype(F32)
        o_ref[:, head_cols(g)] = out.astype(o_ref.dtype)

    q_ts, gates = block_gates(q_next_ref)
    for g in heads:
        select_blocks(g, q_ts[g], gates[g], i + 1)
    scores_stage(s_even_ref, 0, lead_of(i + 1))


def _attn_branch(z, batch, seq):
    nb = seq // MOBA_BLOCK
    n_chunks = nb // KEY_CHUNK_BLOCKS
    gw = HEADS_PER_STEP * HEAD_DIM
    qblk, kblk, vblk, gblk = (c // gw for c in (COL_Q, COL_K, COL_V, COL_GA))
    return pl.pallas_call(
        functools.partial(_attn_kernel, n_blocks=nb),
        grid=(batch, N_HEADS // HEADS_PER_STEP, nb),
        in_specs=[pl.BlockSpec((MOBA_BLOCK, gw), lambda b, h, i: (b * nb, qblk + h)),
                  pl.BlockSpec((MOBA_BLOCK, gw), lambda b, h, i: (b * nb + jnp.minimum(i + 1, nb - 1), qblk + h)),
                  pl.BlockSpec((seq, gw), lambda b, h, i: (b, kblk + h)),
                  pl.BlockSpec((seq, gw), lambda b, h, i: (b, vblk + h)),
                  pl.BlockSpec((MOBA_BLOCK, gw), lambda b, h, i: (b * nb + i, gblk + h))],
        out_specs=pl.BlockSpec((MOBA_BLOCK, gw), lambda b, h, i: (b * nb + i, h)),
        out_shape=jax.ShapeDtypeStruct((batch * seq, ATTN_WIDTH), BF16),
        scratch_shapes=[pltpu.VMEM((n_chunks + 1, KEY_CHUNK, HEAD_DIM), BF16),
                        pltpu.VMEM((HEADS_PER_STEP, n_chunks + 1, VT_ROWS, KEY_CHUNK), BF16),
                        pltpu.VMEM((HEADS_PER_STEP, nb, HEAD_DIM), F32),
                        pltpu.VMEM((HEADS_PER_STEP, 2 * HEAD_DIM, MOBA_BLOCK), BF16),
                        pltpu.VMEM((HEADS_PER_STEP, KEY_CHUNK, MOBA_BLOCK), F32),
                        pltpu.VMEM((HEADS_PER_STEP, KEY_CHUNK, MOBA_BLOCK), F32),
                        pltpu.VMEM((HEADS_PER_STEP, KEY_CHUNK, MOBA_BLOCK), BF16),
                        pltpu.VMEM((HEADS_PER_STEP, KEY_CHUNK, MOBA_BLOCK), BF16),
                        pltpu.VMEM((HEADS_PER_STEP, VT_ROWS, MOBA_BLOCK), F32)],
        compiler_params=_params("arbitrary", "arbitrary", "arbitrary"),
        name="moba_attention",
    )(z, z, z, z, z)


def _merge_kernel(yp_ref, ya_ref, wp_ref, wa_ref, mp_ref, ma_ref, o_ref):
    def matmul(rows, cols):
        return (jnp.dot(yp_ref[rows, :], wp_ref[:, cols], preferred_element_type=F32),
                jnp.dot(ya_ref[rows, :], wa_ref[:, cols], preferred_element_type=F32))

    def finish(rows, cols, acc):
        pool, attn = acc
        merged = mp_ref[rows, cols].astype(F32) * pool + ma_ref[rows, cols].astype(F32) * attn
        o_ref[rows, cols] = merged.astype(o_ref.dtype)

    _shadowed_subtiles(o_ref.shape, matmul, finish)


def _merge(y_pool, y_attn, w_pool_out, w_attn_out, z, tm=1024, tn=1024):
    t = y_pool.shape[0]
    mpblk, mablk = COL_MP // tn, COL_MA // tn
    return pl.pallas_call(
        _merge_kernel,
        grid=(t // tm, D_MODEL // tn),
        in_specs=[pl.BlockSpec((tm, POOL_WIDTH), lambda i, j: (i, 0)),
                  pl.BlockSpec((tm, ATTN_WIDTH), lambda i, j: (i, 0)),
                  pl.BlockSpec((POOL_WIDTH, tn), lambda i, j: (0, j)),
                  pl.BlockSpec((ATTN_WIDTH, tn), lambda i, j: (0, j)),
                  pl.BlockSpec((tm, tn), lambda i, j: (i, mpblk + j)),
                  pl.BlockSpec((tm, tn), lambda i, j: (i, mablk + j))],
        out_specs=pl.BlockSpec((tm, tn), lambda i, j: (i, j)),
        out_shape=jax.ShapeDtypeStruct((t, D_MODEL), BF16),
        compiler_params=_params("arbitrary", "arbitrary"),
        name="gated_merge",
    )(y_pool, y_attn, w_pool_out, w_attn_out, z, z)


def _matmul_kernel(a_ref, b_ref, o_ref):
    _subtiled_matmul(a_ref, b_ref, o_ref, lambda acc: acc)


def _out_proj(a, w, tm=1024, tn=1024):
    t, k = a.shape
    n = w.shape[1]
    return pl.pallas_call(
        _matmul_kernel,
        grid=(t // tm, n // tn),
        in_specs=[pl.BlockSpec((tm, k), lambda i, j: (i, 0)),
                  pl.BlockSpec((k, tn), lambda i, j: (0, j))],
        out_specs=pl.BlockSpec((tm, tn), lambda i, j: (i, j)),
        out_shape=jax.ShapeDtypeStruct((t, n), BF16),
        compiler_params=_params("arbitrary", "arbitrary"),
        name="out_proj",
    )(a, w)


def _ple_kernel(x_ref, t_ref, g_ref, wg_ref, p_ref, wp_ref, o_ref, x1b_ref, rstd_ref, *, tn):
    j = pl.program_id(1)

    @pl.when(j == 0)
    def _():
        t = t_ref[...].astype(F32)
        rstd = lax.rsqrt(jnp.mean(t * t, axis=-1, keepdims=True) + RMS_EPS)
        rstd_ref[...] = rstd
        x1b_ref[...] = (x_ref[...] + t * rstd * g_ref[...]).astype(BF16)

    def matmul(rows, cols):
        return (jnp.dot(x1b_ref[rows, :], wg_ref[:, cols], preferred_element_type=F32),
                jnp.dot(p_ref[rows, :].astype(BF16), wp_ref[:, cols], preferred_element_type=F32))

    def finish(rows, cols, acc):
        gate, emb = acc
        xcols = pl.ds(pl.multiple_of(j * tn + cols.start, MXU_COLS), MXU_COLS)
        x1 = x_ref[rows, xcols] + t_ref[rows, xcols].astype(F32) * rstd_ref[rows, :] * g_ref[:, xcols]
        o_ref[rows, cols] = x1 + _sigmoid(gate) * emb

    _shadowed_subtiles(o_ref.shape, matmul, finish)


def _ple(x2d, t_proj, gain, w_gate, p2d, w_proj, tm=512, tn=512):
    t, d = x2d.shape
    return pl.pallas_call(
        functools.partial(_ple_kernel, tn=tn),
        grid=(t // tm, d // tn),
        in_specs=[pl.BlockSpec((tm, d), lambda i, j: (i, 0)),
                  pl.BlockSpec((tm, d), lambda i, j: (i, 0)),
                  pl.BlockSpec((1, d), lambda i, j: (0, 0)),
                  pl.BlockSpec((d, tn), lambda i, j: (0, j)),
                  pl.BlockSpec((tm, PLE_DIM), lambda i, j: (i, 0)),
                  pl.BlockSpec((PLE_DIM, tn), lambda i, j: (0, j))],
        out_specs=pl.BlockSpec((tm, tn), lambda i, j: (i, j)),
        out_shape=jax.ShapeDtypeStruct((t, d), F32),
        scratch_shapes=[pltpu.VMEM((tm, d), BF16),
                        pltpu.VMEM((tm, 1), F32)],
        compiler_params=_params("arbitrary", "arbitrary"),
        name="ple_residual",
    )(x2d, t_proj, gain, w_gate, p2d, w_proj)


def kernel(x, p, norm_pre, w_in, pool_group_w, pool_scale, w_pool_out, w_attn_out, w_out, norm_post, w_ple_proj, w_ple_gate):
    batch, seq, d = x.shape
    depth = w_in.shape[0]
    assert d == D_MODEL and seq % KEY_CHUNK == 0 and w_in.shape[2] == IN_WIDTH
    x2d = x.reshape(batch * seq, d)
    for layer in range(depth):
        h = _rmsnorm(x2d, norm_pre[layer].reshape(1, d))
        z = _in_proj(h, w_in[layer].astype(BF16))
        y_pool = _pool_branch(z, pool_group_w[layer].astype(BF16), pool_scale[layer].reshape(1, POOL_WIDTH), batch, seq)
        y_attn = _attn_branch(z, batch, seq)
        merged = _merge(y_pool, y_attn, w_pool_out[layer].astype(BF16), w_attn_out[layer].astype(BF16), z)
        t_proj = _out_proj(merged, w_out[layer].astype(BF16))
        x2d = _ple(x2d, t_proj, norm_post[layer].reshape(1, d), w_ple_gate[layer].astype(BF16),
                   p[layer].reshape(batch * seq, PLE_DIM), w_ple_proj[layer].astype(BF16))
    return x2d.reshape(batch, seq, d)
```

```python
import functools

import jax
import jax.numpy as jnp
from jax import lax
from jax.experimental import pallas as pl
from jax.experimental.pallas import tpu as pltpu

F32 = jnp.float32
BF16 = jnp.bfloat16

D_MODEL = 4096
PLE_DIM = 256
POOL_WIDTH = D_MODEL // 2
POOL_WINDOWS = (2, 4, 8, 16)
POOL_GROUP_W = POOL_WIDTH // len(POOL_WINDOWS)
POOL_HALO = 16
HEAD_DIM = 128
ATTN_WIDTH = D_MODEL // 2
N_HEADS = ATTN_WIDTH // HEAD_DIM
MOBA_BLOCK = 256
MOBA_BLOCK_LOG2 = MOBA_BLOCK.bit_length() - 1
assert MOBA_BLOCK == 1 << MOBA_BLOCK_LOG2
MOBA_TOPK = 3
RMS_EPS = 1e-6
IN_WIDTH = 2 * POOL_WIDTH + 4 * ATTN_WIDTH + 2 * D_MODEL

COL_U = 0
COL_GP = COL_U + POOL_WIDTH
COL_Q = COL_GP + POOL_WIDTH
COL_K = COL_Q + ATTN_WIDTH
COL_V = COL_K + ATTN_WIDTH
COL_GA = COL_V + ATTN_WIDTH
COL_MP = COL_GA + ATTN_WIDTH
COL_MA = COL_MP + D_MODEL

KEY_CHUNK_BLOCKS = 2
KEY_CHUNK = KEY_CHUNK_BLOCKS * MOBA_BLOCK
HEADS_PER_STEP = 4
Q_TILE_BLOCKS = 2
Q_TILE = Q_TILE_BLOCKS * MOBA_BLOCK
SOFTMAX_ROWS = 16
VT_ROWS = HEAD_DIM + 16
LOG2_E = 1.4426950408889634
Q_SCALE = HEAD_DIM ** -0.5 * LOG2_E
MASK_NEG = -1e30

V7X_VMEM_LIMIT_BYTES = 56 * 1024 * 1024
LANES = 128
MXU_COLS = 256
MXU_ROWS_PER_DOT = 512


def _params(*semantics):
    return pltpu.CompilerParams(dimension_semantics=semantics,
                                vmem_limit_bytes=V7X_VMEM_LIMIT_BYTES)


def _sigmoid(v):
    return 0.5 * jnp.tanh(0.5 * v) + 0.5


def _silu(v):
    h = 0.5 * v
    return h + h * jnp.tanh(h)


def _rmsnorm_kernel(x_ref, g_ref, o_ref):
    x = x_ref[...]
    ms = jnp.mean(x * x, axis=-1, keepdims=True)
    o_ref[...] = (x * lax.rsqrt(ms + RMS_EPS) * g_ref[...]).astype(o_ref.dtype)


def _rmsnorm(x2d, gain, tm=256):
    t, d = x2d.shape
    return pl.pallas_call(
        _rmsnorm_kernel,
        grid=(t // tm,),
        in_specs=[pl.BlockSpec((tm, d), lambda i: (i, 0)),
                  pl.BlockSpec((1, d), lambda i: (0, 0))],
        out_specs=pl.BlockSpec((tm, d), lambda i: (i, 0)),
        out_shape=jax.ShapeDtypeStruct((t, d), BF16),
        compiler_params=_params("arbitrary"),
        name="rmsnorm_pre",
    )(x2d, gain)


def _shadowed_subtiles(tile_shape, matmul, finish):
    tm, tn = tile_shape
    pending = None
    for c0 in range(0, tn, MXU_COLS):
        for r0 in range(0, tm, min(tm, MXU_ROWS_PER_DOT)):
            rows, cols = slice(r0, r0 + min(tm, MXU_ROWS_PER_DOT)), slice(c0, c0 + MXU_COLS)
            acc = matmul(rows, cols)
            if pending is not None:
                finish(*pending)
            pending = (rows, cols, acc)
    finish(*pending)


def _subtiled_matmul(a_ref, b_ref, o_ref, epilogue):
    def matmul(rows, cols):
        return jnp.dot(a_ref[rows, :], b_ref[:, cols], preferred_element_type=F32)

    def finish(rows, cols, acc):
        o_ref[rows, cols] = epilogue(acc).astype(o_ref.dtype)

    _shadowed_subtiles(o_ref.shape, matmul, finish)


def _win_kernel(h_ref, w_ref, o_ref, *, tn):
    col0 = pl.program_id(1) * tn
    is_silu = ((col0 >= COL_GP) & (col0 < COL_Q)) | ((col0 >= COL_GA) & (col0 < COL_MP))
    is_q = (col0 >= COL_Q) & (col0 < COL_K)
    is_sig = col0 >= COL_MP
    is_plain = jnp.logical_not(is_silu | is_q | is_sig)
    kinds = ((is_plain, lambda acc: acc),
             (is_q, lambda acc: acc * Q_SCALE),
             (is_silu, _silu),
             (is_sig, _sigmoid))
    for pred, epilogue in kinds:
        pl.when(pred)(functools.partial(_subtiled_matmul, h_ref, w_ref, o_ref, epilogue))


def _in_proj(h, w_in, tm=1024, tn=1024):
    t, k = h.shape
    n = w_in.shape[1]
    return pl.pallas_call(
        functools.partial(_win_kernel, tn=tn),
        grid=(t // tm, n // tn),
        in_specs=[pl.BlockSpec((tm, k), lambda i, j: (i, 0)),
                  pl.BlockSpec((k, tn), lambda i, j: (0, j))],
        out_specs=pl.BlockSpec((tm, tn), lambda i, j: (i, j)),
        out_shape=jax.ShapeDtypeStruct((t, n), BF16),
        compiler_params=_params("arbitrary", "arbitrary"),
        name="in_proj",
    )(h, w_in)


def _pool_kernel(u_ref, sg_ref, w_ref, sc_ref, o_ref, hist_ref, *, tm):
    ti = pl.program_id(1)

    @pl.when(ti == 0)
    def _():
        hist_ref[0:POOL_HALO, :] = jnp.zeros((POOL_HALO, POOL_WIDTH), F32)

    @pl.when(ti > 0)
    def _():
        hist_ref[0:POOL_HALO, :] = hist_ref[tm:tm + POOL_HALO, :]

    hist_ref[POOL_HALO:POOL_HALO + tm, :] = u_ref[...].astype(F32)
    pos = ti * tm + lax.broadcasted_iota(jnp.int32, (tm, 1), 0)
    for g, w in enumerate(POOL_WINDOWS):
        c0 = g * POOL_GROUP_W
        cols = pl.ds(c0, POOL_GROUP_W)
        u = hist_ref[pl.ds(POOL_HALO, tm), cols]
        acc = u
        for d in range(1, w):
            acc = acc + hist_ref[pl.ds(POOL_HALO - d, tm), cols]
        count = jnp.minimum(pos + 1, w).astype(F32)
        diff = acc / count - u
        mixed = jnp.dot(diff.astype(BF16), w_ref[g], preferred_element_type=F32)
        y = mixed * sc_ref[:, cols] * sg_ref[:, cols].astype(F32)
        o_ref[:, cols] = y.astype(o_ref.dtype)


def _pool_branch(z, group_w, scale, batch, seq, tm=512):
    nt = seq // tm
    ublk = COL_U // POOL_WIDTH
    gblk = COL_GP // POOL_WIDTH
    return pl.pallas_call(
        functools.partial(_pool_kernel, tm=tm),
        grid=(batch, nt),
        in_specs=[pl.BlockSpec((tm, POOL_WIDTH), lambda b, i: (b * nt + i, ublk)),
                  pl.BlockSpec((tm, POOL_WIDTH), lambda b, i: (b * nt + i, gblk)),
                  pl.BlockSpec(group_w.shape, lambda b, i: (0, 0, 0)),
                  pl.BlockSpec((1, POOL_WIDTH), lambda b, i: (0, 0))],
        out_specs=pl.BlockSpec((tm, POOL_WIDTH), lambda b, i: (b * nt + i, 0)),
        out_shape=jax.ShapeDtypeStruct((batch * seq, POOL_WIDTH), BF16),
        scratch_shapes=[pltpu.VMEM((POOL_HALO + tm, POOL_WIDTH), F32)],
        compiler_params=_params("arbitrary", "arbitrary"),
        name="pool_branch",
    )(z, z, group_w, scale)


def _attn_kernel(q_first_ref, q_next_ref, k_ref, v_ref, sg_ref, o_ref, onehot_ref, vt_ref, kmean_ref, qext_ref,
                 s_even_ref, s_odd_ref, p_even_ref, p_odd_ref, acc_ref, m_ref, a_even_ref, a_odd_ref, causal_ref, *, n_blocks):
    i = pl.program_id(2)
    n_chunks = n_blocks // KEY_CHUNK_BLOCKS
    pad_chunk = n_chunks
    heads = range(HEADS_PER_STEP)

    def head_cols(g):
        return pl.ds(g * HEAD_DIM, HEAD_DIM)

    def chunk_rows(c):
        return pl.ds(pl.multiple_of(c * KEY_CHUNK, KEY_CHUNK), KEY_CHUNK)

    def lead_of(qt):
        return qt % 2

    def chunk_at(pos, lead):
        c = pos - lead
        return jnp.where(c < 0, pad_chunk, c)

    def scores_stage(s_ref, pos, lead):
        c = chunk_at(pos, lead)
        rows = chunk_rows(jnp.minimum(c, n_chunks - 1))
        for g in heads:
            k_ext = jnp.concatenate([k_ref[rows, head_cols(g)], onehot_ref[c]], axis=1)
            for c0 in range(0, Q_TILE, MXU_COLS):
                s_ref[g, :, c0:c0 + MXU_COLS] = jnp.dot(k_ext, qext_ref[g, :, c0:c0 + MXU_COLS],
                                                        preferred_element_type=F32)

    def block_gates(src_ref):
        q_ts = [src_ref[:, head_cols(g)].astype(F32).T.astype(BF16) for g in heads]
        gates = [jnp.dot(kmean_ref[g].astype(BF16), q_ts[g], preferred_element_type=F32) for g in heads]
        return q_ts, gates

    def select_blocks(g, q_t, gate, qt):
        blk_id = lax.broadcasted_iota(jnp.int32, (n_blocks, Q_TILE), 0)
        q_blk = qt * Q_TILE_BLOCKS + (lax.broadcasted_iota(jnp.int32, (n_blocks, Q_TILE), 1) >> MOBA_BLOCK_LOG2)
        is_past = blk_id < q_blk
        past_f = jnp.where(is_past, 1.0, 0.0)
        gt = jnp.where(is_past, gate, -jnp.inf)
        sel = jnp.where(blk_id == q_blk, 1.0, 0.0)
        for _ in range(MOBA_TOPK):
            mx = jnp.max(gt, axis=0, keepdims=True)
            first = jnp.min(jnp.where(gt == mx, blk_id, n_blocks), axis=0, keepdims=True)
            pick = blk_id == first
            sel = jnp.where(pick, jnp.maximum(sel, past_f), sel)
            gt = jnp.where(pick, -jnp.inf, gt)
        qext_ref[g, 0:HEAD_DIM, :] = q_t
        qext_ref[g, HEAD_DIM:HEAD_DIM + n_blocks, :] = jnp.where(sel > 0.5, 0.0, MASK_NEG).astype(BF16)

    @pl.when(i == 0)
    def _prepare_keys_values():
        qext_ref[...] = jnp.zeros(qext_ref.shape, BF16)
        qext_ref[:, HEAD_DIM + n_blocks:HEAD_DIM + n_blocks + 16, :] = jnp.full(
            (HEADS_PER_STEP, 16, Q_TILE), MASK_NEG, BF16)
        lane = lax.broadcasted_iota(jnp.int32, (MOBA_BLOCK, HEAD_DIM), 1)
        pad_lane = lax.broadcasted_iota(jnp.int32, (KEY_CHUNK, HEAD_DIM), 1)
        onehot_ref[pad_chunk] = jnp.where(pad_lane == n_blocks, 1.0, 0.0).astype(BF16)
        vt_ref[:, pad_chunk] = jnp.zeros((HEADS_PER_STEP, VT_ROWS, KEY_CHUNK), BF16)
        vt_ref[:, 0:n_chunks, HEAD_DIM:VT_ROWS, :] = jnp.ones(
            (HEADS_PER_STEP, n_chunks, VT_ROWS - HEAD_DIM, KEY_CHUNK), BF16)
        p_even_ref[...] = jnp.zeros(p_even_ref.shape, BF16)
        key_in_chunk = lax.broadcasted_iota(jnp.int32, (KEY_CHUNK, Q_TILE), 0)
        query_in_tile = lax.broadcasted_iota(jnp.int32, (KEY_CHUNK, Q_TILE), 1)
        causal_ref[...] = jnp.where(key_in_chunk > query_in_tile, MASK_NEG, 0.0).astype(F32)

        def chunk_body(c, carry):
            for s in range(KEY_CHUNK_BLOCKS):
                blk = c * KEY_CHUNK_BLOCKS + s
                rows = pl.ds(pl.multiple_of(blk * MOBA_BLOCK, MOBA_BLOCK), MOBA_BLOCK)
                onehot_ref[c, pl.ds(s * MOBA_BLOCK, MOBA_BLOCK), :] = jnp.where(lane == blk, 1.0, 0.0).astype(BF16)
                for g in heads:
                    kb = k_ref[rows, head_cols(g)].astype(F32)
                    kmean_ref[g, pl.ds(blk, 1), :] = jnp.mean(kb, axis=0, keepdims=True)
            return carry

        lax.fori_loop(0, n_chunks, chunk_body, 0)
        q_ts, gates = block_gates(q_first_ref)
        for g in heads:
            select_blocks(g, q_ts[g], gates[g], i)
        scores_stage(s_even_ref, 0, lead_of(i))

    c_own = i
    lead = lead_of(i)

    def softmax_stage(g, s_ref, p_ref, a_ref, causal=False):
        def scores_rows(r0):
            s_t = s_ref[g, r0:r0 + SOFTMAX_ROWS, :]
            if causal:
                s_t = s_t + causal_ref[r0:r0 + SOFTMAX_ROWS, :]
            return s_t

        row_starts = range(0, KEY_CHUNK, SOFTMAX_ROWS)
        running = scores_rows(0)
        for r0 in row_starts[1:]:
            running = jnp.maximum(running, scores_rows(r0))
        m_old = m_ref[g]
        m_new = jnp.maximum(m_old, jnp.max(running, axis=0, keepdims=True))
        m_ref[g] = m_new
        a_ref[g] = jnp.exp2(m_old - m_new)
        for r0 in row_starts:
            p_ref[g, r0:r0 + SOFTMAX_ROWS, :] = jnp.exp2(scores_rows(r0) - m_new).astype(BF16)

    def value_stage(g, pos, p_ref, a_ref):
        acc_ref[g] = a_ref[g] * acc_ref[g] + jnp.dot(vt_ref[g, chunk_at(pos, lead)], p_ref[g],
                                                     preferred_element_type=F32)

    def pair_body(u, carry):
        pos = 2 * u + 1
        for g in heads:
            value_stage(g, pos - 2, p_even_ref, a_even_ref)
        scores_stage(s_odd_ref, pos, lead)
        for g in heads:
            softmax_stage(g, s_even_ref, p_odd_ref, a_odd_ref)
        scores_stage(s_even_ref, pos + 1, lead)
        for g in heads:
            softmax_stage(g, s_odd_ref, p_even_ref, a_even_ref)
        for g in heads:
            value_stage(g, pos - 1, p_odd_ref, a_odd_ref)
        return carry

    acc_ref[...] = jnp.zeros(acc_ref.shape, F32)
    m_ref[...] = jnp.full(m_ref.shape, MASK_NEG, F32)
    a_even_ref[...] = jnp.ones(a_even_ref.shape, F32)
    last = c_own + lead
    lax.fori_loop(0, last // 2, pair_body, 0)

    for g in heads:
        for s in range(KEY_CHUNK_BLOCKS):
            vb = v_ref[s * MOBA_BLOCK:(s + 1) * MOBA_BLOCK, head_cols(g)].astype(F32)
            vt_ref[g, c_own, 0:HEAD_DIM, s * MOBA_BLOCK:(s + 1) * MOBA_BLOCK] = vb.T.astype(BF16)
    for g in heads:
        softmax_stage(g, s_even_ref, p_odd_ref, a_odd_ref, causal=True)
        value_stage(g, last - 1, p_even_ref, a_even_ref)
        value_stage(g, last, p_odd_ref, a_odd_ref)
        denom = acc_ref[g, HEAD_DIM:HEAD_DIM + 1, :]
        out = (acc_ref[g, 0:HEAD_DIM, :] * (1.0 / denom)).T * sg_ref[:, head_cols(g)].astype(F32)
        o_ref[:, head_cols(g)] = out.astype(o_ref.dtype)

    q_ts, gates = block_gates(q_next_ref)
    for g in heads:
        select_blocks(g, q_ts[g], gates[g], i + 1)
    scores_stage(s_even_ref, 0, lead_of(i + 1))


def _attn_branch(z, batch, seq):
    assert Q_TILE == KEY_CHUNK
    nb = seq // MOBA_BLOCK
    n_chunks = nb // KEY_CHUNK_BLOCKS
    nt = seq // Q_TILE
    gw = HEADS_PER_STEP * HEAD_DIM
    qblk, kblk, vblk, gblk = (c // gw for c in (COL_Q, COL_K, COL_V, COL_GA))
    return pl.pallas_call(
        functools.partial(_attn_kernel, n_blocks=nb),
        grid=(batch, N_HEADS // HEADS_PER_STEP, nt),
        in_specs=[pl.BlockSpec((Q_TILE, gw), lambda b, h, i: (b * nt, qblk + h)),
                  pl.BlockSpec((Q_TILE, gw), lambda b, h, i: (b * nt + jnp.minimum(i + 1, nt - 1), qblk + h)),
                  pl.BlockSpec((seq, gw), lambda b, h, i: (b, kblk + h)),
                  pl.BlockSpec((KEY_CHUNK, gw), lambda b, h, i: (b * nt + i, vblk + h)),
                  pl.BlockSpec((Q_TILE, gw), lambda b, h, i: (b * nt + i, gblk + h))],
        out_specs=pl.BlockSpec((Q_TILE, gw), lambda b, h, i: (b * nt + i, h)),
        out_shape=jax.ShapeDtypeStruct((batch * seq, ATTN_WIDTH), BF16),
        scratch_shapes=[pltpu.VMEM((n_chunks + 1, KEY_CHUNK, HEAD_DIM), BF16),
                        pltpu.VMEM((HEADS_PER_STEP, n_chunks + 1, VT_ROWS, KEY_CHUNK), BF16),
                        pltpu.VMEM((HEADS_PER_STEP, nb, HEAD_DIM), F32),
                        pltpu.VMEM((HEADS_PER_STEP, 2 * HEAD_DIM, Q_TILE), BF16),
                        pltpu.VMEM((HEADS_PER_STEP, KEY_CHUNK, Q_TILE), F32),
                        pltpu.VMEM((HEADS_PER_STEP, KEY_CHUNK, Q_TILE), F32),
                        pltpu.VMEM((HEADS_PER_STEP, KEY_CHUNK, Q_TILE), BF16),
                        pltpu.VMEM((HEADS_PER_STEP, KEY_CHUNK, Q_TILE), BF16),
                        pltpu.VMEM((HEADS_PER_STEP, VT_ROWS, Q_TILE), F32),
                        pltpu.VMEM((HEADS_PER_STEP, 1, Q_TILE), F32),
                        pltpu.VMEM((HEADS_PER_STEP, 1, Q_TILE), F32),
                        pltpu.VMEM((HEADS_PER_STEP, 1, Q_TILE), F32),
                        pltpu.VMEM((KEY_CHUNK, Q_TILE), F32)],
        compiler_params=_params("arbitrary", "arbitrary", "arbitrary"),
        name="moba_attention",
    )(z, z, z, z, z)


def _merge_kernel(yp_ref, ya_ref, wp_ref, wa_ref, mp_ref, ma_ref, o_ref):
    def matmul(rows, cols):
        return (jnp.dot(yp_ref[rows, :], wp_ref[:, cols], preferred_element_type=F32),
                jnp.dot(ya_ref[rows, :], wa_ref[:, cols], preferred_element_type=F32))

    def finish(rows, cols, acc):
        pool, attn = acc
        merged = mp_ref[rows, cols].astype(F32) * pool + ma_ref[rows, cols].astype(F32) * attn
        o_ref[rows, cols] = merged.astype(o_ref.dtype)

    _shadowed_subtiles(o_ref.shape, matmul, finish)


def _merge(y_pool, y_attn, w_pool_out, w_attn_out, z, tm=1024, tn=1024):
    t = y_pool.shape[0]
    mpblk, mablk = COL_MP // tn, COL_MA // tn
    return pl.pallas_call(
        _merge_kernel,
        grid=(t // tm, D_MODEL // tn),
        in_specs=[pl.BlockSpec((tm, POOL_WIDTH), lambda i, j: (i, 0)),
                  pl.BlockSpec((tm, ATTN_WIDTH), lambda i, j: (i, 0)),
                  pl.BlockSpec((POOL_WIDTH, tn), lambda i, j: (0, j)),
                  pl.BlockSpec((ATTN_WIDTH, tn), lambda i, j: (0, j)),
                  pl.BlockSpec((tm, tn), lambda i, j: (i, mpblk + j)),
                  pl.BlockSpec((tm, tn), lambda i, j: (i, mablk + j))],
        out_specs=pl.BlockSpec((tm, tn), lambda i, j: (i, j)),
        out_shape=jax.ShapeDtypeStruct((t, D_MODEL), BF16),
        compiler_params=_params("arbitrary", "arbitrary"),
        name="gated_merge",
    )(y_pool, y_attn, w_pool_out, w_attn_out, z, z)


def _matmul_kernel(a_ref, b_ref, o_ref):
    _subtiled_matmul(a_ref, b_ref, o_ref, lambda acc: acc)


def _out_proj(a, w, tm=1024, tn=1024):
    t, k = a.shape
    n = w.shape[1]
    return pl.pallas_call(
        _matmul_kernel,
        grid=(t // tm, n // tn),
        in_specs=[pl.BlockSpec((tm, k), lambda i, j: (i, 0)),
                  pl.BlockSpec((k, tn), lambda i, j: (0, j))],
        out_specs=pl.BlockSpec((tm, tn), lambda i, j: (i, j)),
        out_shape=jax.ShapeDtypeStruct((t, n), BF16),
        compiler_params=_params("arbitrary", "arbitrary"),
        name="out_proj",
    )(a, w)


def _ple_kernel(x_ref, t_ref, g_ref, wg_ref, p_ref, wp_ref, o_ref, x1b_ref, rstd_ref, *, tn):
    j = pl.program_id(1)

    @pl.when(j == 0)
    def _():
        t = t_ref[...].astype(F32)
        rstd = lax.rsqrt(jnp.mean(t * t, axis=-1, keepdims=True) + RMS_EPS)
        rstd_ref[...] = rstd
        x1b_ref[...] = (x_ref[...] + t * rstd * g_ref[...]).astype(BF16)

    def matmul(rows, cols):
        return (jnp.dot(x1b_ref[rows, :], wg_ref[:, cols], preferred_element_type=F32),
                jnp.dot(p_ref[rows, :].astype(BF16), wp_ref[:, cols], preferred_element_type=F32))

    def finish(rows, cols, acc):
        gate, emb = acc
        xcols = pl.ds(pl.multiple_of(j * tn + cols.start, MXU_COLS), MXU_COLS)
        x1 = x_ref[rows, xcols] + t_ref[rows, xcols].astype(F32) * rstd_ref[rows, :] * g_ref[:, xcols]
        o_ref[rows, cols] = x1 + _sigmoid(gate) * emb

    _shadowed_subtiles(o_ref.shape, matmul, finish)


def _ple(x2d, t_proj, gain, w_gate, p2d, w_proj, tm=512, tn=512):
    t, d = x2d.shape
    return pl.pallas_call(
        functools.partial(_ple_kernel, tn=tn),
        grid=(t // tm, d // tn),
        in_specs=[pl.BlockSpec((tm, d), lambda i, j: (i, 0)),
                  pl.BlockSpec((tm, d), lambda i, j: (i, 0)),
                  pl.BlockSpec((1, d), lambda i, j: (0, 0)),
                  pl.BlockSpec((d, tn), lambda i, j: (0, j)),
                  pl.BlockSpec((tm, PLE_DIM), lambda i, j: (i, 0)),
                  pl.BlockSpec((PLE_DIM, tn), lambda i, j: (0, j))],
        out_specs=pl.BlockSpec((tm, tn), lambda i, j: (i, j)),
        out_shape=jax.ShapeDtypeStruct((t, d), F32),
        scratch_shapes=[pltpu.VMEM((tm, d), BF16),
                        pltpu.VMEM((tm, 1), F32)],
        compiler_params=_params("arbitrary", "arbitrary"),
        name="ple_residual",
    )(x2d, t_proj, gain, w_gate, p2d, w_proj)


def kernel(x, p, norm_pre, w_in, pool_group_w, pool_scale, w_pool_out, w_attn_out, w_out, norm_post, w_ple_proj, w_ple_gate):
    batch, seq, d = x.shape
    depth = w_in.shape[0]
    assert d == D_MODEL and seq % KEY_CHUNK == 0 and w_in.shape[2] == IN_WIDTH
    x2d = x.reshape(batch * seq, d)
    for layer in range(depth):
        h = _rmsnorm(x2d, norm_pre[layer].reshape(1, d))
        z = _in_proj(h, w_in[layer].astype(BF16))
        y_pool = _pool_branch(z, pool_group_w[layer].astype(BF16), pool_scale[layer].reshape(1, POOL_WIDTH), batch, seq)
        y_attn = _attn_branch(z, batch, seq)
        merged = _merge(y_pool, y_attn, w_pool_out[layer].astype(BF16), w_attn_out[layer].astype(BF16), z)
        t_proj = _out_proj(merged, w_out[layer].astype(BF16))
        x2d = _ple(x2d, t_proj, norm_post[layer].reshape(1, d), w_ple_gate[layer].astype(BF16),
                   p[layer].reshape(batch * seq, PLE_DIM), w_ple_proj[layer].astype(BF16))
    return x2d.reshape(batch, seq, d)
```

```python
import functools

import jax
import jax.numpy as jnp
from jax import lax
from jax.experimental import pallas as pl
from jax.experimental.pallas import tpu as pltpu

F32 = jnp.float32
BF16 = jnp.bfloat16

D_MODEL = 4096
PLE_DIM = 256
POOL_WIDTH = D_MODEL // 2
POOL_WINDOWS = (2, 4, 8, 16)
POOL_GROUP_W = POOL_WIDTH // len(POOL_WINDOWS)
assert all(w == 1 << (w.bit_length() - 1) for w in POOL_WINDOWS)
POOL_HALO = 8 * (max(POOL_WINDOWS).bit_length() - 1)
HEAD_DIM = 128
ATTN_WIDTH = D_MODEL // 2
N_HEADS = ATTN_WIDTH // HEAD_DIM
MOBA_BLOCK = 256
MOBA_BLOCK_LOG2 = MOBA_BLOCK.bit_length() - 1
assert MOBA_BLOCK == 1 << MOBA_BLOCK_LOG2
MOBA_TOPK = 3
RMS_EPS = 1e-6
IN_WIDTH = 2 * POOL_WIDTH + 4 * ATTN_WIDTH + 2 * D_MODEL

COL_U = 0
COL_GP = COL_U + POOL_WIDTH
COL_Q = COL_GP + POOL_WIDTH
COL_K = COL_Q + ATTN_WIDTH
COL_V = COL_K + ATTN_WIDTH
COL_GA = COL_V + ATTN_WIDTH
COL_MP = COL_GA + ATTN_WIDTH
COL_MA = COL_MP + D_MODEL

KEY_CHUNK_BLOCKS = 2
KEY_CHUNK = KEY_CHUNK_BLOCKS * MOBA_BLOCK
HEADS_PER_STEP = 4
Q_TILE_BLOCKS = 2
Q_TILE = Q_TILE_BLOCKS * MOBA_BLOCK
SOFTMAX_ROWS = 16
VT_ROWS = HEAD_DIM + 16
LOG2_E = 1.4426950408889634
Q_SCALE = HEAD_DIM ** -0.5 * LOG2_E
MASK_NEG = -1e30

V7X_VMEM_LIMIT_BYTES = 56 * 1024 * 1024
LANES = 128
NORM_ROWS = 16
MXU_COLS = 256
MXU_ROWS_PER_DOT = 512


def _params(*semantics):
    return pltpu.CompilerParams(dimension_semantics=semantics,
                                vmem_limit_bytes=V7X_VMEM_LIMIT_BYTES)


def _sigmoid(v):
    return 0.5 * jnp.tanh(0.5 * v) + 0.5


def _silu(v):
    h = 0.5 * v
    return h + h * jnp.tanh(h)


def _rmsnorm_kernel(x_ref, g_ref, o_ref):
    x = x_ref[...]
    ms = jnp.mean(x * x, axis=-1, keepdims=True)
    o_ref[...] = (x * lax.rsqrt(ms + RMS_EPS) * g_ref[...]).astype(o_ref.dtype)


def _rmsnorm(x2d, gain, tm=256):
    t, d = x2d.shape
    return pl.pallas_call(
        _rmsnorm_kernel,
        grid=(t // tm,),
        in_specs=[pl.BlockSpec((tm, d), lambda i: (i, 0)),
                  pl.BlockSpec((1, d), lambda i: (0, 0))],
        out_specs=pl.BlockSpec((tm, d), lambda i: (i, 0)),
        out_shape=jax.ShapeDtypeStruct((t, d), BF16),
        compiler_params=_params("arbitrary"),
        name="rmsnorm_pre",
    )(x2d, gain)


def _shadowed_subtiles(tile_shape, matmul, finish):
    tm, tn = tile_shape
    pending = None
    for c0 in range(0, tn, MXU_COLS):
        for r0 in range(0, tm, min(tm, MXU_ROWS_PER_DOT)):
            rows, cols = slice(r0, r0 + min(tm, MXU_ROWS_PER_DOT)), slice(c0, c0 + MXU_COLS)
            acc = matmul(rows, cols)
            if pending is not None:
                finish(*pending)
            pending = (rows, cols, acc)
    finish(*pending)


def _subtiled_matmul(a_ref, b_ref, o_ref, epilogue):
    def matmul(rows, cols):
        return jnp.dot(a_ref[rows, :], b_ref[:, cols], preferred_element_type=F32)

    def finish(rows, cols, acc):
        o_ref[rows, cols] = epilogue(acc).astype(o_ref.dtype)

    _shadowed_subtiles(o_ref.shape, matmul, finish)


def _win_kernel(h_ref, w_ref, o_ref, *, tn):
    col0 = pl.program_id(1) * tn
    is_silu = ((col0 >= COL_GP) & (col0 < COL_Q)) | ((col0 >= COL_GA) & (col0 < COL_MP))
    is_q = (col0 >= COL_Q) & (col0 < COL_K)
    is_sig = col0 >= COL_MP
    is_plain = jnp.logical_not(is_silu | is_q | is_sig)
    kinds = ((is_plain, lambda acc: acc),
             (is_q, lambda acc: acc * Q_SCALE),
             (is_silu, _silu),
             (is_sig, _sigmoid))
    for pred, epilogue in kinds:
        pl.when(pred)(functools.partial(_subtiled_matmul, h_ref, w_ref, o_ref, epilogue))


def _in_proj(h, w_in, tm=1024, tn=1024):
    t, k = h.shape
    n = w_in.shape[1]
    return pl.pallas_call(
        functools.partial(_win_kernel, tn=tn),
        grid=(t // tm, n // tn),
        in_specs=[pl.BlockSpec((tm, k), lambda i, j: (i, 0)),
                  pl.BlockSpec((k, tn), lambda i, j: (0, j))],
        out_specs=pl.BlockSpec((tm, tn), lambda i, j: (i, j)),
        out_shape=jax.ShapeDtypeStruct((t, n), BF16),
        compiler_params=_params("arbitrary", "arbitrary"),
        name="in_proj",
    )(h, w_in)


def _pool_kernel(u_ref, sg_ref, w_ref, sc_ref, o_ref, hist_ref, lvl_a_ref, lvl_b_ref, *, tm):
    ti = pl.program_id(1)
    total = POOL_HALO + tm

    @pl.when(ti == 0)
    def _():
        hist_ref[0:POOL_HALO, :] = jnp.zeros((POOL_HALO, POOL_WIDTH), F32)

    @pl.when(ti > 0)
    def _():
        hist_ref[0:POOL_HALO, :] = hist_ref[tm:tm + POOL_HALO, :]

    hist_ref[POOL_HALO:POOL_HALO + tm, :] = u_ref[...].astype(F32)
    pos = ti * tm + lax.broadcasted_iota(jnp.int32, (tm, 1), 0)
    for g, w in enumerate(POOL_WINDOWS):
        c0 = g * POOL_GROUP_W
        cols = pl.ds(c0, POOL_GROUP_W)
        u = hist_ref[pl.ds(POOL_HALO, tm), cols]
        src_ref, src_cols = hist_ref, cols
        n_levels = w.bit_length() - 1
        for k in range(n_levels):
            lo = 8 * (k + 1)
            level = (src_ref[pl.ds(lo, total - lo), src_cols]
                     + src_ref[pl.ds(lo - (1 << k), total - lo), src_cols])
            if k + 1 < n_levels:
                dst_ref = lvl_a_ref if k % 2 == 0 else lvl_b_ref
                dst_ref[pl.ds(lo, total - lo), :] = level
                src_ref, src_cols = dst_ref, slice(None)
        acc = level[POOL_HALO - lo:, :]
        count = jnp.minimum(pos + 1, w).astype(F32)
        diff = acc / count - u
        mixed = jnp.dot(diff.astype(BF16), w_ref[g], preferred_element_type=F32)
        y = mixed * sc_ref[:, cols] * sg_ref[:, cols].astype(F32)
        o_ref[:, cols] = y.astype(o_ref.dtype)


def _pool_branch(z, group_w, scale, batch, seq, tm=512):
    nt = seq // tm
    ublk = COL_U // POOL_WIDTH
    gblk = COL_GP // POOL_WIDTH
    return pl.pallas_call(
        functools.partial(_pool_kernel, tm=tm),
        grid=(batch, nt),
        in_specs=[pl.BlockSpec((tm, POOL_WIDTH), lambda b, i: (b * nt + i, ublk)),
                  pl.BlockSpec((tm, POOL_WIDTH), lambda b, i: (b * nt + i, gblk)),
                  pl.BlockSpec(group_w.shape, lambda b, i: (0, 0, 0)),
                  pl.BlockSpec((1, POOL_WIDTH), lambda b, i: (0, 0))],
        out_specs=pl.BlockSpec((tm, POOL_WIDTH), lambda b, i: (b * nt + i, 0)),
        out_shape=jax.ShapeDtypeStruct((batch * seq, POOL_WIDTH), BF16),
        scratch_shapes=[pltpu.VMEM((POOL_HALO + tm, POOL_WIDTH), F32),
                        pltpu.VMEM((POOL_HALO + tm, POOL_GROUP_W), F32),
                        pltpu.VMEM((POOL_HALO + tm, POOL_GROUP_W), F32)],
        compiler_params=_params("arbitrary", "arbitrary"),
        name="pool_branch",
    )(z, z, group_w, scale)


def _attn_kernel(q_first_ref, q_next_ref, k_ref, v_ref, sg_ref, o_ref, onehot_ref, vt_ref, kmean_ref, qext_ref,
                 s_even_ref, s_odd_ref, p_even_ref, p_odd_ref, acc_ref, m_ref, a_even_ref, a_odd_ref, causal_ref, *, n_blocks):
    i = pl.program_id(2)
    n_chunks = n_blocks // KEY_CHUNK_BLOCKS
    pad_chunk = n_chunks
    heads = range(HEADS_PER_STEP)

    def head_cols(g):
        return pl.ds(g * HEAD_DIM, HEAD_DIM)

    def chunk_rows(c):
        return pl.ds(pl.multiple_of(c * KEY_CHUNK, KEY_CHUNK), KEY_CHUNK)

    def lead_of(qt):
        return qt % 2

    def chunk_at(pos, lead):
        c = pos - lead
        return jnp.where(c < 0, pad_chunk, c)

    def scores_stage(s_ref, pos, lead, which_heads=heads):
        c = chunk_at(pos, lead)
        rows = chunk_rows(jnp.minimum(c, n_chunks - 1))
        for g in which_heads:
            k_ext = jnp.concatenate([k_ref[rows, head_cols(g)], onehot_ref[c]], axis=1)
            for c0 in range(0, Q_TILE, MXU_COLS):
                s_ref[g, :, c0:c0 + MXU_COLS] = jnp.dot(k_ext, qext_ref[g, :, c0:c0 + MXU_COLS],
                                                        preferred_element_type=F32)

    def block_gates(src_ref):
        q_ts = [src_ref[:, head_cols(g)].astype(F32).T.astype(BF16) for g in heads]
        gates = [jnp.dot(kmean_ref[g].astype(BF16), q_ts[g], preferred_element_type=F32) for g in heads]
        return q_ts, gates

    def select_blocks(g, q_t, gate, qt):
        blk_id = lax.broadcasted_iota(jnp.int32, (n_blocks, Q_TILE), 0)
        q_blk = qt * Q_TILE_BLOCKS + (lax.broadcasted_iota(jnp.int32, (n_blocks, Q_TILE), 1) >> MOBA_BLOCK_LOG2)
        is_past = blk_id < q_blk
        past_f = jnp.where(is_past, 1.0, 0.0)
        gt = jnp.where(is_past, gate, -jnp.inf)
        sel = jnp.where(blk_id == q_blk, 1.0, 0.0)
        for _ in range(MOBA_TOPK):
            mx = jnp.max(gt, axis=0, keepdims=True)
            first = jnp.min(jnp.where(gt == mx, blk_id, n_blocks), axis=0, keepdims=True)
            pick = blk_id == first
            sel = jnp.where(pick, jnp.maximum(sel, past_f), sel)
            gt = jnp.where(pick, -jnp.inf, gt)
        qext_ref[g, 0:HEAD_DIM, :] = q_t
        qext_ref[g, HEAD_DIM:HEAD_DIM + n_blocks, :] = jnp.where(sel > 0.5, 0.0, MASK_NEG).astype(BF16)

    @pl.when(i == 0)
    def _prepare_keys_values():
        qext_ref[...] = jnp.zeros(qext_ref.shape, BF16)
        qext_ref[:, HEAD_DIM + n_blocks:HEAD_DIM + n_blocks + 16, :] = jnp.full(
            (HEADS_PER_STEP, 16, Q_TILE), MASK_NEG, BF16)
        lane = lax.broadcasted_iota(jnp.int32, (MOBA_BLOCK, HEAD_DIM), 1)
        pad_lane = lax.broadcasted_iota(jnp.int32, (KEY_CHUNK, HEAD_DIM), 1)
        onehot_ref[pad_chunk] = jnp.where(pad_lane == n_blocks, 1.0, 0.0).astype(BF16)
        vt_ref[:, pad_chunk] = jnp.zeros((HEADS_PER_STEP, VT_ROWS, KEY_CHUNK), BF16)
        vt_ref[:, 0:n_chunks, HEAD_DIM:VT_ROWS, :] = jnp.ones(
            (HEADS_PER_STEP, n_chunks, VT_ROWS - HEAD_DIM, KEY_CHUNK), BF16)
        p_even_ref[...] = jnp.zeros(p_even_ref.shape, BF16)
        key_in_chunk = lax.broadcasted_iota(jnp.int32, (KEY_CHUNK, Q_TILE), 0)
        query_in_tile = lax.broadcasted_iota(jnp.int32, (KEY_CHUNK, Q_TILE), 1)
        causal_ref[...] = jnp.where(key_in_chunk > query_in_tile, MASK_NEG, 0.0).astype(F32)

        def chunk_body(c, carry):
            for s in range(KEY_CHUNK_BLOCKS):
                blk = c * KEY_CHUNK_BLOCKS + s
                rows = pl.ds(pl.multiple_of(blk * MOBA_BLOCK, MOBA_BLOCK), MOBA_BLOCK)
                onehot_ref[c, pl.ds(s * MOBA_BLOCK, MOBA_BLOCK), :] = jnp.where(lane == blk, 1.0, 0.0).astype(BF16)
                for g in heads:
                    kb = k_ref[rows, head_cols(g)].astype(F32)
                    kmean_ref[g, pl.ds(blk, 1), :] = jnp.mean(kb, axis=0, keepdims=True)
            return carry

        lax.fori_loop(0, n_chunks, chunk_body, 0)
        q_ts, gates = block_gates(q_first_ref)
        for g in heads:
            select_blocks(g, q_ts[g], gates[g], i)
        scores_stage(s_even_ref, 0, lead_of(i))

    c_own = i
    lead = lead_of(i)

    def softmax_stage(g, s_ref, p_ref, a_ref, causal=False):
        def scores_rows(r0):
            s_t = s_ref[g, r0:r0 + SOFTMAX_ROWS, :]
            if causal:
                s_t = s_t + causal_ref[r0:r0 + SOFTMAX_ROWS, :]
            return s_t

        row_starts = range(0, KEY_CHUNK, SOFTMAX_ROWS)
        running = scores_rows(0)
        for r0 in row_starts[1:]:
            running = jnp.maximum(running, scores_rows(r0))
        m_old = m_ref[g]
        m_new = jnp.maximum(m_old, jnp.max(running, axis=0, keepdims=True))
        m_ref[g] = m_new
        a_ref[g] = jnp.exp2(m_old - m_new)
        for r0 in row_starts:
            p_ref[g, r0:r0 + SOFTMAX_ROWS, :] = jnp.exp2(scores_rows(r0) - m_new).astype(BF16)

    def value_stage(g, pos, p_ref, a_ref):
        acc_ref[g] = a_ref[g] * acc_ref[g] + jnp.dot(vt_ref[g, chunk_at(pos, lead)], p_ref[g],
                                                     preferred_element_type=F32)

    def pair_body(u, carry):
        pos = 2 * u + 1
        for g in heads:
            value_stage(g, pos - 2, p_even_ref, a_even_ref)
        scores_stage(s_odd_ref, pos, lead)
        for g in heads:
            softmax_stage(g, s_even_ref, p_odd_ref, a_odd_ref)
        scores_stage(s_even_ref, pos + 1, lead)
        for g in heads:
            softmax_stage(g, s_odd_ref, p_even_ref, a_even_ref)
        for g in heads:
            value_stage(g, pos - 1, p_odd_ref, a_odd_ref)
        return carry

    acc_ref[...] = jnp.zeros(acc_ref.shape, F32)
    m_ref[...] = jnp.full(m_ref.shape, MASK_NEG, F32)
    a_even_ref[...] = jnp.ones(a_even_ref.shape, F32)
    last = c_own + lead
    lax.fori_loop(0, last // 2, pair_body, 0)

    for g in heads:
        for s in range(KEY_CHUNK_BLOCKS):
            vb = v_ref[s * MOBA_BLOCK:(s + 1) * MOBA_BLOCK, head_cols(g)].astype(F32)
            vt_ref[g, c_own, 0:HEAD_DIM, s * MOBA_BLOCK:(s + 1) * MOBA_BLOCK] = vb.T.astype(BF16)
    q_ts, gates = block_gates(q_next_ref)
    for g in heads:
        select_blocks(g, q_ts[g], gates[g], i + 1)
    for g in heads:
        softmax_stage(g, s_even_ref, p_odd_ref, a_odd_ref, causal=True)
        value_stage(g, last - 1, p_even_ref, a_even_ref)
        value_stage(g, last, p_odd_ref, a_odd_ref)
        scores_stage(s_even_ref, 0, lead_of(i + 1), which_heads=(g,))
        denom = acc_ref[g, HEAD_DIM:HEAD_DIM + 1, :]
        out = (acc_ref[g, 0:HEAD_DIM, :] * (1.0 / denom)).T * sg_ref[:, head_cols(g)].astype(F32)
        o_ref[:, head_cols(g)] = out.astype(o_ref.dtype)


def _attn_branch(z, batch, seq):
    assert Q_TILE == KEY_CHUNK
    nb = seq // MOBA_BLOCK
    n_chunks = nb // KEY_CHUNK_BLOCKS
    nt = seq // Q_TILE
    gw = HEADS_PER_STEP * HEAD_DIM
    qblk, kblk, vblk, gblk = (c // gw for c in (COL_Q, COL_K, COL_V, COL_GA))
    return pl.pallas_call(
        functools.partial(_attn_kernel, n_blocks=nb),
        grid=(batch, N_HEADS // HEADS_PER_STEP, nt),
        in_specs=[pl.BlockSpec((Q_TILE, gw), lambda b, h, i: (b * nt, qblk + h)),
                  pl.BlockSpec((Q_TILE, gw), lambda b, h, i: (b * nt + jnp.minimum(i + 1, nt - 1), qblk + h)),
                  pl.BlockSpec((seq, gw), lambda b, h, i: (b, kblk + h)),
                  pl.BlockSpec((KEY_CHUNK, gw), lambda b, h, i: (b * nt + i, vblk + h)),
                  pl.BlockSpec((Q_TILE, gw), lambda b, h, i: (b * nt + i, gblk + h))],
        out_specs=pl.BlockSpec((Q_TILE, gw), lambda b, h, i: (b * nt + i, h)),
        out_shape=jax.ShapeDtypeStruct((batch * seq, ATTN_WIDTH), BF16),
        scratch_shapes=[pltpu.VMEM((n_chunks + 1, KEY_CHUNK, HEAD_DIM), BF16),
                        pltpu.VMEM((HEADS_PER_STEP, n_chunks + 1, VT_ROWS, KEY_CHUNK), BF16),
                        pltpu.VMEM((HEADS_PER_STEP, nb, HEAD_DIM), F32),
                        pltpu.VMEM((HEADS_PER_STEP, 2 * HEAD_DIM, Q_TILE), BF16),
                        pltpu.VMEM((HEADS_PER_STEP, KEY_CHUNK, Q_TILE), F32),
                        pltpu.VMEM((HEADS_PER_STEP, KEY_CHUNK, Q_TILE), F32),
                        pltpu.VMEM((HEADS_PER_STEP, KEY_CHUNK, Q_TILE), BF16),
                        pltpu.VMEM((HEADS_PER_STEP, KEY_CHUNK, Q_TILE), BF16),
                        pltpu.VMEM((HEADS_PER_STEP, VT_ROWS, Q_TILE), F32),
                        pltpu.VMEM((HEADS_PER_STEP, 1, Q_TILE), F32),
                        pltpu.VMEM((HEADS_PER_STEP, 1, Q_TILE), F32),
                        pltpu.VMEM((HEADS_PER_STEP, 1, Q_TILE), F32),
                        pltpu.VMEM((KEY_CHUNK, Q_TILE), F32)],
        compiler_params=_params("arbitrary", "arbitrary", "arbitrary"),
        name="moba_attention",
    )(z, z, z, z, z)


def _merge_kernel(yp_ref, ya_ref, wp_ref, wa_ref, mp_ref, ma_ref, o_ref):
    def matmul(rows, cols):
        return (jnp.dot(yp_ref[rows, :], wp_ref[:, cols], preferred_element_type=F32),
                jnp.dot(ya_ref[rows, :], wa_ref[:, cols], preferred_element_type=F32))

    def finish(rows, cols, acc):
        pool, attn = acc
        merged = mp_ref[rows, cols].astype(F32) * pool + ma_ref[rows, cols].astype(F32) * attn
        o_ref[rows, cols] = merged.astype(o_ref.dtype)

    _shadowed_subtiles(o_ref.shape, matmul, finish)


def _merge(y_pool, y_attn, w_pool_out, w_attn_out, z, tm=1024, tn=1024):
    t = y_pool.shape[0]
    mpblk, mablk = COL_MP // tn, COL_MA // tn
    return pl.pallas_call(
        _merge_kernel,
        grid=(t // tm, D_MODEL // tn),
        in_specs=[pl.BlockSpec((tm, POOL_WIDTH), lambda i, j: (i, 0)),
                  pl.BlockSpec((tm, ATTN_WIDTH), lambda i, j: (i, 0)),
                  pl.BlockSpec((POOL_WIDTH, tn), lambda i, j: (0, j)),
                  pl.BlockSpec((ATTN_WIDTH, tn), lambda i, j: (0, j)),
                  pl.BlockSpec((tm, tn), lambda i, j: (i, mpblk + j)),
                  pl.BlockSpec((tm, tn), lambda i, j: (i, mablk + j))],
        out_specs=pl.BlockSpec((tm, tn), lambda i, j: (i, j)),
        out_shape=jax.ShapeDtypeStruct((t, D_MODEL), BF16),
        compiler_params=_params("arbitrary", "arbitrary"),
        name="gated_merge",
    )(y_pool, y_attn, w_pool_out, w_attn_out, z, z)


def _matmul_kernel(a_ref, b_ref, o_ref):
    _subtiled_matmul(a_ref, b_ref, o_ref, lambda acc: acc)


def _out_proj(a, w, tm=1024, tn=1024):
    t, k = a.shape
    n = w.shape[1]
    return pl.pallas_call(
        _matmul_kernel,
        grid=(t // tm, n // tn),
        in_specs=[pl.BlockSpec((tm, k), lambda i, j: (i, 0)),
                  pl.BlockSpec((k, tn), lambda i, j: (0, j))],
        out_specs=pl.BlockSpec((tm, tn), lambda i, j: (i, j)),
        out_shape=jax.ShapeDtypeStruct((t, n), BF16),
        compiler_params=_params("arbitrary", "arbitrary"),
        name="out_proj",
    )(a, w)


def _ple_kernel(x_ref, t_ref, g_ref, wg_ref, p_ref, wp_ref, o_ref, x1b_ref, rstd_ref, *, tn):
    j = pl.program_id(1)

    @pl.when(j == 0)
    def _():
        def row_block(r, carry):
            rows = pl.ds(pl.multiple_of(r * NORM_ROWS, NORM_ROWS), NORM_ROWS)
            t = t_ref[rows, :].astype(F32)
            rstd = lax.rsqrt(jnp.mean(t * t, axis=-1, keepdims=True) + RMS_EPS)
            rstd_ref[rows, :] = rstd
            x1b_ref[rows, :] = (x_ref[rows, :] + t * rstd * g_ref[...]).astype(BF16)
            return carry

        lax.fori_loop(0, x_ref.shape[0] // NORM_ROWS, row_block, 0, unroll=4)

    def matmul(rows, cols):
        return (jnp.dot(x1b_ref[rows, :], wg_ref[:, cols], preferred_element_type=F32),
                jnp.dot(p_ref[rows, :].astype(BF16), wp_ref[:, cols], preferred_element_type=F32))

    def finish(rows, cols, acc):
        gate, emb = acc
        xcols = pl.ds(pl.multiple_of(j * tn + cols.start, MXU_COLS), MXU_COLS)
        x1 = x_ref[rows, xcols] + t_ref[rows, xcols].astype(F32) * rstd_ref[rows, :] * g_ref[:, xcols]
        o_ref[rows, cols] = x1 + _sigmoid(gate) * emb

    _shadowed_subtiles(o_ref.shape, matmul, finish)


def _ple(x2d, t_proj, gain, w_gate, p2d, w_proj, tm=512, tn=1024):
    t, d = x2d.shape
    return pl.pallas_call(
        functools.partial(_ple_kernel, tn=tn),
        grid=(t // tm, d // tn),
        in_specs=[pl.BlockSpec((tm, d), lambda i, j: (i, 0)),
                  pl.BlockSpec((tm, d), lambda i, j: (i, 0)),
                  pl.BlockSpec((1, d), lambda i, j: (0, 0)),
                  pl.BlockSpec((d, tn), lambda i, j: (0, j)),
                  pl.BlockSpec((tm, PLE_DIM), lambda i, j: (i, 0)),
                  pl.BlockSpec((PLE_DIM, tn), lambda i, j: (0, j))],
        out_specs=pl.BlockSpec((tm, tn), lambda i, j: (i, j)),
        out_shape=jax.ShapeDtypeStruct((t, d), F32),
        scratch_shapes=[pltpu.VMEM((tm, d), BF16),
                        pltpu.VMEM((tm, 1), F32)],
        compiler_params=_params("arbitrary", "arbitrary"),
        name="ple_residual",
    )(x2d, t_proj, gain, w_gate, p2d, w_proj)


def kernel(x, p, norm_pre, w_in, pool_group_w, pool_scale, w_pool_out, w_attn_out, w_out, norm_post, w_ple_proj, w_ple_gate):
    batch, seq, d = x.shape
    depth = w_in.shape[0]
    assert d == D_MODEL and seq % KEY_CHUNK == 0 and w_in.shape[2] == IN_WIDTH
    x2d = x.reshape(batch * seq, d)
    for layer in range(depth):
        h = _rmsnorm(x2d, norm_pre[layer].reshape(1, d))
        z = _in_proj(h, w_in[layer].astype(BF16))
        y_pool = _pool_branch(z, pool_group_w[layer].astype(BF16), pool_scale[layer].reshape(1, POOL_WIDTH), batch, seq)
        y_attn = _attn_branch(z, batch, seq)
        merged = _merge(y_pool, y_attn, w_pool_out[layer].astype(BF16), w_attn_out[layer].astype(BF16), z)
        t_proj = _out_proj(merged, w_out[layer].astype(BF16))
        x2d = _ple(x2d, t_proj, norm_post[layer].reshape(1, d), w_ple_gate[layer].astype(BF16),
                   p[layer].reshape(batch * seq, PLE_DIM), w_ple_proj[layer].astype(BF16))
    return x2d.reshape(batch, seq, d)
```

```python
import functools

import jax
import jax.numpy as jnp
from jax import lax
from jax.experimental import pallas as pl
from jax.experimental.pallas import tpu as pltpu

F32 = jnp.float32
BF16 = jnp.bfloat16

D_MODEL = 4096
PLE_DIM = 256
POOL_WIDTH = D_MODEL // 2
POOL_WINDOWS = (2, 4, 8, 16)
POOL_GROUP_W = POOL_WIDTH // len(POOL_WINDOWS)
assert all(w == 1 << (w.bit_length() - 1) for w in POOL_WINDOWS)
POOL_HALO = 8 * (max(POOL_WINDOWS).bit_length() - 1)
HEAD_DIM = 128
ATTN_WIDTH = D_MODEL // 2
N_HEADS = ATTN_WIDTH // HEAD_DIM
MOBA_BLOCK = 256
MOBA_BLOCK_LOG2 = MOBA_BLOCK.bit_length() - 1
assert MOBA_BLOCK == 1 << MOBA_BLOCK_LOG2
MOBA_TOPK = 3
RMS_EPS = 1e-6
IN_WIDTH = 2 * POOL_WIDTH + 4 * ATTN_WIDTH + 2 * D_MODEL

COL_U = 0
COL_GP = COL_U + POOL_WIDTH
COL_Q = COL_GP + POOL_WIDTH
COL_K = COL_Q + ATTN_WIDTH
COL_V = COL_K + ATTN_WIDTH
COL_GA = COL_V + ATTN_WIDTH
COL_MP = COL_GA + ATTN_WIDTH
COL_MA = COL_MP + D_MODEL

KEY_CHUNK_BLOCKS = 2
KEY_CHUNK = KEY_CHUNK_BLOCKS * MOBA_BLOCK
HEADS_PER_STEP = 4
Q_TILE_BLOCKS = 2
Q_TILE = Q_TILE_BLOCKS * MOBA_BLOCK
SOFTMAX_ROWS = 16
VT_ROWS = HEAD_DIM + 16
LOG2_E = 1.4426950408889634
Q_SCALE = HEAD_DIM ** -0.5 * LOG2_E
MASK_NEG = -1e30

V7X_VMEM_LIMIT_BYTES = 56 * 1024 * 1024
LANES = 128
CAST_ROWS = 16
NORM_ROWS = 16
MXU_COLS = 256
MXU_ROWS_PER_DOT = 512


def _params(*semantics):
    return pltpu.CompilerParams(dimension_semantics=semantics,
                                vmem_limit_bytes=V7X_VMEM_LIMIT_BYTES)


def _sigmoid(v):
    return 0.5 * jnp.tanh(0.5 * v) + 0.5


def _silu(v):
    h = 0.5 * v
    return h + h * jnp.tanh(h)


def _rmsnorm_kernel(x_ref, g_ref, o_ref):
    x = x_ref[...]
    ms = jnp.mean(x * x, axis=-1, keepdims=True)
    o_ref[...] = (x * lax.rsqrt(ms + RMS_EPS) * g_ref[...]).astype(o_ref.dtype)


def _rmsnorm(x2d, gain, tm=256):
    t, d = x2d.shape
    return pl.pallas_call(
        _rmsnorm_kernel,
        grid=(t // tm,),
        in_specs=[pl.BlockSpec((tm, d), lambda i: (i, 0)),
                  pl.BlockSpec((1, d), lambda i: (0, 0))],
        out_specs=pl.BlockSpec((tm, d), lambda i: (i, 0)),
        out_shape=jax.ShapeDtypeStruct((t, d), BF16),
        compiler_params=_params("arbitrary"),
        name="rmsnorm_pre",
    )(x2d, gain)


def _shadowed_subtiles(tile_shape, matmul, finish):
    tm, tn = tile_shape
    pending = None
    for c0 in range(0, tn, MXU_COLS):
        for r0 in range(0, tm, min(tm, MXU_ROWS_PER_DOT)):
            rows, cols = slice(r0, r0 + min(tm, MXU_ROWS_PER_DOT)), slice(c0, c0 + MXU_COLS)
            acc = matmul(rows, cols)
            if pending is not None:
                finish(*pending)
            pending = (rows, cols, acc)
    finish(*pending)


def _subtiled_matmul(a_ref, b_ref, o_ref, epilogue):
    def matmul(rows, cols):
        return jnp.dot(a_ref[rows, :], b_ref[:, cols], preferred_element_type=F32)

    def finish(rows, cols, acc):
        o_ref[rows, cols] = epilogue(acc).astype(o_ref.dtype)

    _shadowed_subtiles(o_ref.shape, matmul, finish)


def _win_kernel(h_ref, w_ref, *refs, tn, n_side):
    side_in, o_ref, side_out = refs[:n_side], refs[n_side], refs[n_side + 1:]
    for src_ref, dst_ref in zip(side_in, side_out):
        dst_ref[...] = src_ref[...].astype(dst_ref.dtype)
    col0 = pl.program_id(1) * tn
    is_silu = ((col0 >= COL_GP) & (col0 < COL_Q)) | ((col0 >= COL_GA) & (col0 < COL_MP))
    is_q = (col0 >= COL_Q) & (col0 < COL_K)
    is_sig = col0 >= COL_MP
    is_plain = jnp.logical_not(is_silu | is_q | is_sig)
    kinds = ((is_plain, lambda acc: acc),
             (is_q, lambda acc: acc * Q_SCALE),
             (is_silu, _silu),
             (is_sig, _sigmoid))
    for pred, epilogue in kinds:
        pl.when(pred)(functools.partial(_subtiled_matmul, h_ref, w_ref, o_ref, epilogue))


def _in_proj(h, w_in, side_weights, tm=1024, tn=1024):
    t, k = h.shape
    n = w_in.shape[1]
    n_j = n // tn
    steps = (t // tm) * n_j
    side_specs, side_shapes = [], []
    for w in side_weights:
        rows, cols = w.shape
        assert rows % CAST_ROWS == 0 and rows // CAST_ROWS <= steps
        last = rows // CAST_ROWS - 1
        spec = pl.BlockSpec((CAST_ROWS, cols), lambda i, j, last=last: (jnp.minimum(i * n_j + j, last), 0))
        side_specs.append(spec)
        side_shapes.append(jax.ShapeDtypeStruct(w.shape, BF16))
    outs = pl.pallas_call(
        functools.partial(_win_kernel, tn=tn, n_side=len(side_weights)),
        grid=(t // tm, n_j),
        in_specs=[pl.BlockSpec((tm, k), lambda i, j: (i, 0)),
                  pl.BlockSpec((k, tn), lambda i, j: (0, j))] + side_specs,
        out_specs=[pl.BlockSpec((tm, tn), lambda i, j: (i, j))] + side_specs,
        out_shape=[jax.ShapeDtypeStruct((t, n), BF16)] + side_shapes,
        compiler_params=_params("arbitrary", "arbitrary"),
        name="in_proj",
    )(h, w_in, *side_weights)
    return outs[0], outs[1:]


def _pool_kernel(u_ref, sg_ref, w_ref, sc_ref, o_ref, hist_ref, lvl_a_ref, lvl_b_ref, *, tm):
    ti = pl.program_id(1)
    total = POOL_HALO + tm

    @pl.when(ti == 0)
    def _():
        hist_ref[0:POOL_HALO, :] = jnp.zeros((POOL_HALO, POOL_WIDTH), F32)

    @pl.when(ti > 0)
    def _():
        hist_ref[0:POOL_HALO, :] = hist_ref[tm:tm + POOL_HALO, :]

    hist_ref[POOL_HALO:POOL_HALO + tm, :] = u_ref[...].astype(F32)
    pos = ti * tm + lax.broadcasted_iota(jnp.int32, (tm, 1), 0)
    for g, w in enumerate(POOL_WINDOWS):
        c0 = g * POOL_GROUP_W
        cols = pl.ds(c0, POOL_GROUP_W)
        u = hist_ref[pl.ds(POOL_HALO, tm), cols]
        src_ref, src_cols = hist_ref, cols
        n_levels = w.bit_length() - 1
        for k in range(n_levels):
            lo = 8 * (k + 1)
            level = (src_ref[pl.ds(lo, total - lo), src_cols]
                     + src_ref[pl.ds(lo - (1 << k), total - lo), src_cols])
            if k + 1 < n_levels:
                dst_ref = lvl_a_ref if k % 2 == 0 else lvl_b_ref
                dst_ref[pl.ds(lo, total - lo), :] = level
                src_ref, src_cols = dst_ref, slice(None)
        acc = level[POOL_HALO - lo:, :]
        count = jnp.minimum(pos + 1, w).astype(F32)
        diff = acc / count - u
        mixed = jnp.dot(diff.astype(BF16), w_ref[g], preferred_element_type=F32)
        y = mixed * sc_ref[:, cols] * sg_ref[:, cols].astype(F32)
        o_ref[:, cols] = y.astype(o_ref.dtype)


def _pool_branch(z, group_w, scale, batch, seq, tm=512):
    nt = seq // tm
    ublk = COL_U // POOL_WIDTH
    gblk = COL_GP // POOL_WIDTH
    return pl.pallas_call(
        functools.partial(_pool_kernel, tm=tm),
        grid=(batch, nt),
        in_specs=[pl.BlockSpec((tm, POOL_WIDTH), lambda b, i: (b * nt + i, ublk)),
                  pl.BlockSpec((tm, POOL_WIDTH), lambda b, i: (b * nt + i, gblk)),
                  pl.BlockSpec(group_w.shape, lambda b, i: (0, 0, 0)),
                  pl.BlockSpec((1, POOL_WIDTH), lambda b, i: (0, 0))],
        out_specs=pl.BlockSpec((tm, POOL_WIDTH), lambda b, i: (b * nt + i, 0)),
        out_shape=jax.ShapeDtypeStruct((batch * seq, POOL_WIDTH), BF16),
        scratch_shapes=[pltpu.VMEM((POOL_HALO + tm, POOL_WIDTH), F32),
                        pltpu.VMEM((POOL_HALO + tm, POOL_GROUP_W), F32),
                        pltpu.VMEM((POOL_HALO + tm, POOL_GROUP_W), F32)],
        compiler_params=_params("arbitrary", "arbitrary"),
        name="pool_branch",
    )(z, z, group_w, scale)


def _attn_kernel(q_first_ref, q_next_ref, k_ref, v_ref, sg_ref, o_ref, onehot_ref, vt_ref, kmean_ref, qext_ref,
                 s_even_ref, s_odd_ref, p_even_ref, p_odd_ref, acc_ref, m_ref, a_even_ref, a_odd_ref, causal_ref, *, n_blocks):
    i = pl.program_id(2)
    n_chunks = n_blocks // KEY_CHUNK_BLOCKS
    pad_chunk = n_chunks
    heads = range(HEADS_PER_STEP)

    def head_cols(g):
        return pl.ds(g * HEAD_DIM, HEAD_DIM)

    def chunk_rows(c):
        return pl.ds(pl.multiple_of(c * KEY_CHUNK, KEY_CHUNK), KEY_CHUNK)

    def lead_of(qt):
        return qt % 2

    def chunk_at(pos, lead):
        c = pos - lead
        return jnp.where(c < 0, pad_chunk, c)

    def scores_stage(s_ref, pos, lead, which_heads=heads):
        c = chunk_at(pos, lead)
        rows = chunk_rows(jnp.minimum(c, n_chunks - 1))
        for g in which_heads:
            k_ext = jnp.concatenate([k_ref[rows, head_cols(g)], onehot_ref[c]], axis=1)
            for c0 in range(0, Q_TILE, MXU_COLS):
                s_ref[g, :, c0:c0 + MXU_COLS] = jnp.dot(k_ext, qext_ref[g, :, c0:c0 + MXU_COLS],
                                                        preferred_element_type=F32)

    def block_gates(src_ref):
        q_ts = [src_ref[:, head_cols(g)].astype(F32).T.astype(BF16) for g in heads]
        gates = [jnp.dot(kmean_ref[g].astype(BF16), q_ts[g], preferred_element_type=F32) for g in heads]
        return q_ts, gates

    def select_blocks(g, q_t, gate, qt):
        blk_id = lax.broadcasted_iota(jnp.int32, (n_blocks, Q_TILE), 0)
        q_blk = qt * Q_TILE_BLOCKS + (lax.broadcasted_iota(jnp.int32, (n_blocks, Q_TILE), 1) >> MOBA_BLOCK_LOG2)
        is_past = blk_id < q_blk
        past_f = jnp.where(is_past, 1.0, 0.0)
        gt = jnp.where(is_past, gate, -jnp.inf)
        sel = jnp.where(blk_id == q_blk, 1.0, 0.0)
        for _ in range(MOBA_TOPK):
            mx = jnp.max(gt, axis=0, keepdims=True)
            first = jnp.min(jnp.where(gt == mx, blk_id, n_blocks), axis=0, keepdims=True)
            pick = blk_id == first
            sel = jnp.where(pick, jnp.maximum(sel, past_f), sel)
            gt = jnp.where(pick, -jnp.inf, gt)
        qext_ref[g, 0:HEAD_DIM, :] = q_t
        qext_ref[g, HEAD_DIM:HEAD_DIM + n_blocks, :] = jnp.where(sel > 0.5, 0.0, MASK_NEG).astype(BF16)

    @pl.when(i == 0)
    def _prepare_keys_values():
        qext_ref[...] = jnp.zeros(qext_ref.shape, BF16)
        qext_ref[:, HEAD_DIM + n_blocks:HEAD_DIM + n_blocks + 16, :] = jnp.full(
            (HEADS_PER_STEP, 16, Q_TILE), MASK_NEG, BF16)
        lane = lax.broadcasted_iota(jnp.int32, (MOBA_BLOCK, HEAD_DIM), 1)
        pad_lane = lax.broadcasted_iota(jnp.int32, (KEY_CHUNK, HEAD_DIM), 1)
        onehot_ref[pad_chunk] = jnp.where(pad_lane == n_blocks, 1.0, 0.0).astype(BF16)
        vt_ref[:, pad_chunk] = jnp.zeros((HEADS_PER_STEP, VT_ROWS, KEY_CHUNK), BF16)
        vt_ref[:, 0:n_chunks, HEAD_DIM:VT_ROWS, :] = jnp.ones(
            (HEADS_PER_STEP, n_chunks, VT_ROWS - HEAD_DIM, KEY_CHUNK), BF16)
        p_even_ref[...] = jnp.zeros(p_even_ref.shape, BF16)
        key_in_chunk = lax.broadcasted_iota(jnp.int32, (KEY_CHUNK, Q_TILE), 0)
        query_in_tile = lax.broadcasted_iota(jnp.int32, (KEY_CHUNK, Q_TILE), 1)
        causal_ref[...] = jnp.where(key_in_chunk > query_in_tile, MASK_NEG, 0.0).astype(F32)

        def chunk_body(c, carry):
            for s in range(KEY_CHUNK_BLOCKS):
                blk = c * KEY_CHUNK_BLOCKS + s
                rows = pl.ds(pl.multiple_of(blk * MOBA_BLOCK, MOBA_BLOCK), MOBA_BLOCK)
                onehot_ref[c, pl.ds(s * MOBA_BLOCK, MOBA_BLOCK), :] = jnp.where(lane == blk, 1.0, 0.0).astype(BF16)
                for g in heads:
                    kb = k_ref[rows, head_cols(g)].astype(F32)
                    kmean_ref[g, pl.ds(blk, 1), :] = jnp.mean(kb, axis=0, keepdims=True)
            return carry

        lax.fori_loop(0, n_chunks, chunk_body, 0)
        q_ts, gates = block_gates(q_first_ref)
        for g in heads:
            select_blocks(g, q_ts[g], gates[g], i)
        scores_stage(s_even_ref, 0, lead_of(i))

    c_own = i
    lead = lead_of(i)

    def softmax_stage(g, s_ref, p_ref, a_ref, causal=False):
        def scores_rows(r0):
            s_t = s_ref[g, r0:r0 + SOFTMAX_ROWS, :]
            if causal:
                s_t = s_t + causal_ref[r0:r0 + SOFTMAX_ROWS, :]
            return s_t

        row_starts = range(0, KEY_CHUNK, SOFTMAX_ROWS)
        running = scores_rows(0)
        for r0 in row_starts[1:]:
            running = jnp.maximum(running, scores_rows(r0))
        m_old = m_ref[g]
        m_new = jnp.maximum(m_old, jnp.max(running, axis=0, keepdims=True))
        m_ref[g] = m_new
        a_ref[g] = jnp.exp2(m_old - m_new)
        for r0 in row_starts:
            p_ref[g, r0:r0 + SOFTMAX_ROWS, :] = jnp.exp2(scores_rows(r0) - m_new).astype(BF16)

    def value_stage(g, pos, p_ref, a_ref):
        acc_ref[g] = a_ref[g] * acc_ref[g] + jnp.dot(vt_ref[g, chunk_at(pos, lead)], p_ref[g],
                                                     preferred_element_type=F32)

    def pair_body(u, carry):
        pos = 2 * u + 1
        for g in heads:
            value_stage(g, pos - 2, p_even_ref, a_even_ref)
        scores_stage(s_odd_ref, pos, lead)
        for g in heads:
            softmax_stage(g, s_even_ref, p_odd_ref, a_odd_ref)
        scores_stage(s_even_ref, pos + 1, lead)
        for g in heads:
            softmax_stage(g, s_odd_ref, p_even_ref, a_even_ref)
        for g in heads:
            value_stage(g, pos - 1, p_odd_ref, a_odd_ref)
        return carry

    acc_ref[...] = jnp.zeros(acc_ref.shape, F32)
    m_ref[...] = jnp.full(m_ref.shape, MASK_NEG, F32)
    a_even_ref[...] = jnp.ones(a_even_ref.shape, F32)
    last = c_own + lead
    lax.fori_loop(0, last // 2, pair_body, 0)

    for g in heads:
        for s in range(KEY_CHUNK_BLOCKS):
            vb = v_ref[s * MOBA_BLOCK:(s + 1) * MOBA_BLOCK, head_cols(g)].astype(F32)
            vt_ref[g, c_own, 0:HEAD_DIM, s * MOBA_BLOCK:(s + 1) * MOBA_BLOCK] = vb.T.astype(BF16)
    q_ts, gates = block_gates(q_next_ref)
    for g in heads:
        select_blocks(g, q_ts[g], gates[g], i + 1)
    for g in heads:
        softmax_stage(g, s_even_ref, p_odd_ref, a_odd_ref, causal=True)
        value_stage(g, last - 1, p_even_ref, a_even_ref)
        value_stage(g, last, p_odd_ref, a_odd_ref)
        scores_stage(s_even_ref, 0, lead_of(i + 1), which_heads=(g,))
        denom = acc_ref[g, HEAD_DIM:HEAD_DIM + 1, :]
        out = (acc_ref[g, 0:HEAD_DIM, :] * (1.0 / denom)).T * sg_ref[:, head_cols(g)].astype(F32)
        o_ref[:, head_cols(g)] = out.astype(o_ref.dtype)


def _attn_branch(z, batch, seq):
    assert Q_TILE == KEY_CHUNK
    nb = seq // MOBA_BLOCK
    n_chunks = nb // KEY_CHUNK_BLOCKS
    nt = seq // Q_TILE
    gw = HEADS_PER_STEP * HEAD_DIM
    qblk, kblk, vblk, gblk = (c // gw for c in (COL_Q, COL_K, COL_V, COL_GA))
    return pl.pallas_call(
        functools.partial(_attn_kernel, n_blocks=nb),
        grid=(batch, N_HEADS // HEADS_PER_STEP, nt),
        in_specs=[pl.BlockSpec((Q_TILE, gw), lambda b, h, i: (b * nt, qblk + h)),
                  pl.BlockSpec((Q_TILE, gw), lambda b, h, i: (b * nt + jnp.minimum(i + 1, nt - 1), qblk + h)),
                  pl.BlockSpec((seq, gw), lambda b, h, i: (b, kblk + h)),
                  pl.BlockSpec((KEY_CHUNK, gw), lambda b, h, i: (b * nt + i, vblk + h)),
                  pl.BlockSpec((Q_TILE, gw), lambda b, h, i: (b * nt + i, gblk + h))],
        out_specs=pl.BlockSpec((Q_TILE, gw), lambda b, h, i: (b * nt + i, h)),
        out_shape=jax.ShapeDtypeStruct((batch * seq, ATTN_WIDTH), BF16),
        scratch_shapes=[pltpu.VMEM((n_chunks + 1, KEY_CHUNK, HEAD_DIM), BF16),
                        pltpu.VMEM((HEADS_PER_STEP, n_chunks + 1, VT_ROWS, KEY_CHUNK), BF16),
                        pltpu.VMEM((HEADS_PER_STEP, nb, HEAD_DIM), F32),
                        pltpu.VMEM((HEADS_PER_STEP, 2 * HEAD_DIM, Q_TILE), BF16),
                        pltpu.VMEM((HEADS_PER_STEP, KEY_CHUNK, Q_TILE), F32),
                        pltpu.VMEM((HEADS_PER_STEP, KEY_CHUNK, Q_TILE), F32),
                        pltpu.VMEM((HEADS_PER_STEP, KEY_CHUNK, Q_TILE), BF16),
                        pltpu.VMEM((HEADS_PER_STEP, KEY_CHUNK, Q_TILE), BF16),
                        pltpu.VMEM((HEADS_PER_STEP, VT_ROWS, Q_TILE), F32),
                        pltpu.VMEM((HEADS_PER_STEP, 1, Q_TILE), F32),
                        pltpu.VMEM((HEADS_PER_STEP, 1, Q_TILE), F32),
                        pltpu.VMEM((HEADS_PER_STEP, 1, Q_TILE), F32),
                        pltpu.VMEM((KEY_CHUNK, Q_TILE), F32)],
        compiler_params=_params("arbitrary", "arbitrary", "arbitrary"),
        name="moba_attention",
    )(z, z, z, z, z)


def _merge_kernel(yp_ref, ya_ref, wp_ref, wa_ref, mp_ref, ma_ref, o_ref):
    def matmul(rows, cols):
        return (jnp.dot(yp_ref[rows, :], wp_ref[:, cols], preferred_element_type=F32),
                jnp.dot(ya_ref[rows, :], wa_ref[:, cols], preferred_element_type=F32))

    def finish(rows, cols, acc):
        pool, attn = acc
        merged = mp_ref[rows, cols].astype(F32) * pool + ma_ref[rows, cols].astype(F32) * attn
        o_ref[rows, cols] = merged.astype(o_ref.dtype)

    _shadowed_subtiles(o_ref.shape, matmul, finish)


def _merge(y_pool, y_attn, w_pool_out, w_attn_out, z, tm=1024, tn=1024):
    t = y_pool.shape[0]
    mpblk, mablk = COL_MP // tn, COL_MA // tn
    return pl.pallas_call(
        _merge_kernel,
        grid=(t // tm, D_MODEL // tn),
        in_specs=[pl.BlockSpec((tm, POOL_WIDTH), lambda i, j: (i, 0)),
                  pl.BlockSpec((tm, ATTN_WIDTH), lambda i, j: (i, 0)),
                  pl.BlockSpec((POOL_WIDTH, tn), lambda i, j: (0, j)),
                  pl.BlockSpec((ATTN_WIDTH, tn), lambda i, j: (0, j)),
                  pl.BlockSpec((tm, tn), lambda i, j: (i, mpblk + j)),
                  pl.BlockSpec((tm, tn), lambda i, j: (i, mablk + j))],
        out_specs=pl.BlockSpec((tm, tn), lambda i, j: (i, j)),
        out_shape=jax.ShapeDtypeStruct((t, D_MODEL), BF16),
        compiler_params=_params("arbitrary", "arbitrary"),
        name="gated_merge",
    )(y_pool, y_attn, w_pool_out, w_attn_out, z, z)


def _matmul_kernel(a_ref, b_ref, o_ref):
    _subtiled_matmul(a_ref, b_ref, o_ref, lambda acc: acc)


def _out_proj(a, w, tm=1024, tn=1024):
    t, k = a.shape
    n = w.shape[1]
    return pl.pallas_call(
        _matmul_kernel,
        grid=(t // tm, n // tn),
        in_specs=[pl.BlockSpec((tm, k), lambda i, j: (i, 0)),
                  pl.BlockSpec((k, tn), lambda i, j: (0, j))],
        out_specs=pl.BlockSpec((tm, tn), lambda i, j: (i, j)),
        out_shape=jax.ShapeDtypeStruct((t, n), BF16),
        compiler_params=_params("arbitrary", "arbitrary"),
        name="out_proj",
    )(a, w)


def _ple_kernel(x_ref, t_ref, g_ref, wg_ref, p_ref, wp_ref, o_ref, x1b_ref, rstd_ref, *, tn):
    j = pl.program_id(1)

    @pl.when(j == 0)
    def _():
        def row_block(r, carry):
            rows = pl.ds(pl.multiple_of(r * NORM_ROWS, NORM_ROWS), NORM_ROWS)
            t = t_ref[rows, :].astype(F32)
            rstd = lax.rsqrt(jnp.mean(t * t, axis=-1, keepdims=True) + RMS_EPS)
            rstd_ref[rows, :] = rstd
            x1b_ref[rows, :] = (x_ref[rows, :] + t * rstd * g_ref[...]).astype(BF16)
            return carry

        lax.fori_loop(0, x_ref.shape[0] // NORM_ROWS, row_block, 0, unroll=4)

    def matmul(rows, cols):
        return (jnp.dot(x1b_ref[rows, :], wg_ref[:, cols], preferred_element_type=F32),
                jnp.dot(p_ref[rows, :].astype(BF16), wp_ref[:, cols], preferred_element_type=F32))

    def finish(rows, cols, acc):
        gate, emb = acc
        xcols = pl.ds(pl.multiple_of(j * tn + cols.start, MXU_COLS), MXU_COLS)
        x1 = x_ref[rows, xcols] + t_ref[rows, xcols].astype(F32) * rstd_ref[rows, :] * g_ref[:, xcols]
        o_ref[rows, cols] = x1 + _sigmoid(gate) * emb

    _shadowed_subtiles(o_ref.shape, matmul, finish)


def _ple(x2d, t_proj, gain, w_gate, p2d, w_proj, tm=512, tn=1024):
    t, d = x2d.shape
    return pl.pallas_call(
        functools.partial(_ple_kernel, tn=tn),
        grid=(t // tm, d // tn),
        in_specs=[pl.BlockSpec((tm, d), lambda i, j: (i, 0)),
                  pl.BlockSpec((tm, d), lambda i, j: (i, 0)),
                  pl.BlockSpec((1, d), lambda i, j: (0, 0)),
                  pl.BlockSpec((d, tn), lambda i, j: (0, j)),
                  pl.BlockSpec((tm, PLE_DIM), lambda i, j: (i, 0)),
                  pl.BlockSpec((PLE_DIM, tn), lambda i, j: (0, j))],
        out_specs=pl.BlockSpec((tm, tn), lambda i, j: (i, j)),
        out_shape=jax.ShapeDtypeStruct((t, d), F32),
        scratch_shapes=[pltpu.VMEM((tm, d), BF16),
                        pltpu.VMEM((tm, 1), F32)],
        compiler_params=_params("arbitrary", "arbitrary"),
        name="ple_residual",
    )(x2d, t_proj, gain, w_gate, p2d, w_proj)


def kernel(x, p, norm_pre, w_in, pool_group_w, pool_scale, w_pool_out, w_attn_out, w_out, norm_post, w_ple_proj, w_ple_gate):
    batch, seq, d = x.shape
    depth = w_in.shape[0]
    assert d == D_MODEL and seq % KEY_CHUNK == 0 and w_in.shape[2] == IN_WIDTH
    x2d = x.reshape(batch * seq, d)
    for layer in range(depth):
        h = _rmsnorm(x2d, norm_pre[layer].reshape(1, d))
        z, (w_pool_b, w_attn_b, w_out_b, w_gate_b) = _in_proj(
            h, w_in[layer].astype(BF16),
            (w_pool_out[layer], w_attn_out[layer], w_out[layer], w_ple_gate[layer]))
        y_pool = _pool_branch(z, pool_group_w[layer].astype(BF16), pool_scale[layer].reshape(1, POOL_WIDTH), batch, seq)
        y_attn = _attn_branch(z, batch, seq)
        merged = _merge(y_pool, y_attn, w_pool_b, w_attn_b, z)
        t_proj = _out_proj(merged, w_out_b)
        x2d = _ple(x2d, t_proj, norm_post[layer].reshape(1, d), w_gate_b,
                   p[layer].reshape(batch * seq, PLE_DIM), w_ple_proj[layer].astype(BF16))
    return x2d.reshape(batch, seq, d)
```

```python
import functools

import jax
import jax.numpy as jnp
from jax import lax
from jax.experimental import pallas as pl
from jax.experimental.pallas import tpu as pltpu

F32 = jnp.float32
BF16 = jnp.bfloat16

D_MODEL = 4096
PLE_DIM = 256
POOL_WIDTH = D_MODEL // 2
POOL_WINDOWS = (2, 4, 8, 16)
POOL_GROUP_W = POOL_WIDTH // len(POOL_WINDOWS)
assert all(w == 1 << (w.bit_length() - 1) for w in POOL_WINDOWS)
POOL_HALO = 8 * (max(POOL_WINDOWS).bit_length() - 1)
HEAD_DIM = 128
ATTN_WIDTH = D_MODEL // 2
N_HEADS = ATTN_WIDTH // HEAD_DIM
MOBA_BLOCK = 256
MOBA_BLOCK_LOG2 = MOBA_BLOCK.bit_length() - 1
assert MOBA_BLOCK == 1 << MOBA_BLOCK_LOG2
MOBA_TOPK = 3
RMS_EPS = 1e-6
IN_WIDTH = 2 * POOL_WIDTH + 4 * ATTN_WIDTH + 2 * D_MODEL

COL_U = 0
COL_GP = COL_U + POOL_WIDTH
COL_Q = COL_GP + POOL_WIDTH
COL_K = COL_Q + ATTN_WIDTH
COL_V = COL_K + ATTN_WIDTH
COL_GA = COL_V + ATTN_WIDTH
COL_MP = COL_GA + ATTN_WIDTH
COL_MA = COL_MP + D_MODEL

KEY_CHUNK_BLOCKS = 2
KEY_CHUNK = KEY_CHUNK_BLOCKS * MOBA_BLOCK
HEADS_PER_STEP = 4
Q_TILE_BLOCKS = 2
Q_TILE = Q_TILE_BLOCKS * MOBA_BLOCK
SOFTMAX_ROWS = 16
VT_ROWS = HEAD_DIM + 16
LOG2_E = 1.4426950408889634
Q_SCALE = HEAD_DIM ** -0.5 * LOG2_E
MASK_NEG = -1e30

V7X_VMEM_LIMIT_BYTES = 56 * 1024 * 1024
LANES = 128
CAST_ROWS = 16
NORM_ROWS = 16
MXU_COLS = 256
MXU_ROWS_PER_DOT = 512


def _params(*semantics):
    return pltpu.CompilerParams(dimension_semantics=semantics,
                                vmem_limit_bytes=V7X_VMEM_LIMIT_BYTES)


def _sigmoid(v):
    return 0.5 * jnp.tanh(0.5 * v) + 0.5


def _silu(v):
    h = 0.5 * v
    return h + h * jnp.tanh(h)


def _rmsnorm_kernel(x_ref, g_ref, o_ref):
    x = x_ref[...]
    ms = jnp.mean(x * x, axis=-1, keepdims=True)
    o_ref[...] = (x * lax.rsqrt(ms + RMS_EPS) * g_ref[...]).astype(o_ref.dtype)


def _rmsnorm(x2d, gain, tm=512):
    t, d = x2d.shape
    return pl.pallas_call(
        _rmsnorm_kernel,
        grid=(t // tm,),
        in_specs=[pl.BlockSpec((tm, d), lambda i: (i, 0)),
                  pl.BlockSpec((1, d), lambda i: (0, 0))],
        out_specs=pl.BlockSpec((tm, d), lambda i: (i, 0)),
        out_shape=jax.ShapeDtypeStruct((t, d), BF16),
        compiler_params=_params("arbitrary"),
        name="rmsnorm_pre",
    )(x2d, gain)


def _shadowed_subtiles(tile_shape, matmul, finish):
    tm, tn = tile_shape
    pending = None
    for c0 in range(0, tn, MXU_COLS):
        for r0 in range(0, tm, min(tm, MXU_ROWS_PER_DOT)):
            rows, cols = slice(r0, r0 + min(tm, MXU_ROWS_PER_DOT)), slice(c0, c0 + MXU_COLS)
            acc = matmul(rows, cols)
            if pending is not None:
                finish(*pending)
            pending = (rows, cols, acc)
    finish(*pending)


def _subtiled_matmul(a_ref, b_ref, o_ref, epilogue):
    def matmul(rows, cols):
        return jnp.dot(a_ref[rows, :], b_ref[:, cols], preferred_element_type=F32)

    def finish(rows, cols, acc):
        o_ref[rows, cols] = epilogue(acc).astype(o_ref.dtype)

    _shadowed_subtiles(o_ref.shape, matmul, finish)


def _win_kernel(h_ref, w_ref, *refs, tn, n_side):
    side_in, o_ref, side_out = refs[:n_side], refs[n_side], refs[n_side + 1:]
    for src_ref, dst_ref in zip(side_in, side_out):
        dst_ref[...] = src_ref[...].astype(dst_ref.dtype)
    col0 = pl.program_id(1) * tn
    is_silu = ((col0 >= COL_GP) & (col0 < COL_Q)) | ((col0 >= COL_GA) & (col0 < COL_MP))
    is_q = (col0 >= COL_Q) & (col0 < COL_K)
    is_sig = col0 >= COL_MP
    is_plain = jnp.logical_not(is_silu | is_q | is_sig)
    kinds = ((is_plain, lambda acc: acc),
             (is_q, lambda acc: acc * Q_SCALE),
             (is_silu, _silu),
             (is_sig, _sigmoid))
    for pred, epilogue in kinds:
        pl.when(pred)(functools.partial(_subtiled_matmul, h_ref, w_ref, o_ref, epilogue))


def _in_proj(h, w_in, side_weights, tm=1024, tn=1024):
    t, k = h.shape
    n = w_in.shape[1]
    n_j = n // tn
    steps = (t // tm) * n_j
    side_specs, side_shapes = [], []
    for w in side_weights:
        rows, cols = w.shape
        assert rows % CAST_ROWS == 0 and rows // CAST_ROWS <= steps
        last = rows // CAST_ROWS - 1
        spec = pl.BlockSpec((CAST_ROWS, cols), lambda i, j, last=last: (jnp.minimum(i * n_j + j, last), 0))
        side_specs.append(spec)
        side_shapes.append(jax.ShapeDtypeStruct(w.shape, BF16))
    outs = pl.pallas_call(
        functools.partial(_win_kernel, tn=tn, n_side=len(side_weights)),
        grid=(t // tm, n_j),
        in_specs=[pl.BlockSpec((tm, k), lambda i, j: (i, 0)),
                  pl.BlockSpec((k, tn), lambda i, j: (0, j))] + side_specs,
        out_specs=[pl.BlockSpec((tm, tn), lambda i, j: (i, j))] + side_specs,
        out_shape=[jax.ShapeDtypeStruct((t, n), BF16)] + side_shapes,
        compiler_params=_params("arbitrary", "arbitrary"),
        name="in_proj",
    )(h, w_in, *side_weights)
    return outs[0], outs[1:]


def _pool_kernel(u_ref, sg_ref, w_ref, sc_ref, o_ref, hist_ref, lvl_a_ref, lvl_b_ref, *, tm):
    ti = pl.program_id(1)
    total = POOL_HALO + tm

    @pl.when(ti == 0)
    def _():
        hist_ref[0:POOL_HALO, :] = jnp.zeros((POOL_HALO, POOL_WIDTH), F32)

    @pl.when(ti > 0)
    def _():
        hist_ref[0:POOL_HALO, :] = hist_ref[tm:tm + POOL_HALO, :]

    hist_ref[POOL_HALO:POOL_HALO + tm, :] = u_ref[...].astype(F32)
    pos = ti * tm + lax.broadcasted_iota(jnp.int32, (tm, 1), 0)
    for g, w in enumerate(POOL_WINDOWS):
        c0 = g * POOL_GROUP_W
        cols = pl.ds(c0, POOL_GROUP_W)
        u = hist_ref[pl.ds(POOL_HALO, tm), cols]
        src_ref, src_cols = hist_ref, cols
        n_levels = w.bit_length() - 1
        for k in range(n_levels):
            lo = 8 * (k + 1)
            level = (src_ref[pl.ds(lo, total - lo), src_cols]
                     + src_ref[pl.ds(lo - (1 << k), total - lo), src_cols])
            if k + 1 < n_levels:
                dst_ref = lvl_a_ref if k % 2 == 0 else lvl_b_ref
                dst_ref[pl.ds(lo, total - lo), :] = level
                src_ref, src_cols = dst_ref, slice(None)
        acc = level[POOL_HALO - lo:, :]
        count = jnp.minimum(pos + 1, w).astype(F32)
        diff = acc / count - u
        mixed = jnp.dot(diff.astype(BF16), w_ref[g], preferred_element_type=F32)
        y = mixed * sc_ref[:, cols] * sg_ref[:, cols].astype(F32)
        o_ref[:, cols] = y.astype(o_ref.dtype)


def _pool_branch(z, group_w, scale, batch, seq, tm=512):
    nt = seq // tm
    ublk = COL_U // POOL_WIDTH
    gblk = COL_GP // POOL_WIDTH
    return pl.pallas_call(
        functools.partial(_pool_kernel, tm=tm),
        grid=(batch, nt),
        in_specs=[pl.BlockSpec((tm, POOL_WIDTH), lambda b, i: (b * nt + i, ublk)),
                  pl.BlockSpec((tm, POOL_WIDTH), lambda b, i: (b * nt + i, gblk)),
                  pl.BlockSpec(group_w.shape, lambda b, i: (0, 0, 0)),
                  pl.BlockSpec((1, POOL_WIDTH), lambda b, i: (0, 0))],
        out_specs=pl.BlockSpec((tm, POOL_WIDTH), lambda b, i: (b * nt + i, 0)),
        out_shape=jax.ShapeDtypeStruct((batch * seq, POOL_WIDTH), BF16),
        scratch_shapes=[pltpu.VMEM((POOL_HALO + tm, POOL_WIDTH), F32),
                        pltpu.VMEM((POOL_HALO + tm, POOL_GROUP_W), F32),
                        pltpu.VMEM((POOL_HALO + tm, POOL_GROUP_W), F32)],
        compiler_params=_params("arbitrary", "arbitrary"),
        name="pool_branch",
    )(z, z, group_w, scale)


def _attn_kernel(q_first_ref, q_next_ref, k_ref, v_ref, sg_ref, o_ref, onehot_ref, vt_ref, kmean_ref, qext_ref,
                 s_even_ref, s_odd_ref, p_even_ref, p_odd_ref, acc_ref, m_ref, a_even_ref, a_odd_ref, causal_ref, *, n_blocks):
    i = pl.program_id(2)
    n_chunks = n_blocks // KEY_CHUNK_BLOCKS
    pad_chunk = n_chunks
    heads = range(HEADS_PER_STEP)

    def head_cols(g):
        return pl.ds(g * HEAD_DIM, HEAD_DIM)

    def chunk_rows(c):
        return pl.ds(pl.multiple_of(c * KEY_CHUNK, KEY_CHUNK), KEY_CHUNK)

    def lead_of(qt):
        return qt % 2

    def chunk_at(pos, lead):
        c = pos - lead
        return jnp.where(c < 0, pad_chunk, c)

    def scores_stage(s_ref, pos, lead, which_heads=heads):
        c = chunk_at(pos, lead)
        rows = chunk_rows(jnp.minimum(c, n_chunks - 1))
        for g in which_heads:
            k_ext = jnp.concatenate([k_ref[rows, head_cols(g)], onehot_ref[c]], axis=1)
            for c0 in range(0, Q_TILE, MXU_COLS):
                s_ref[g, :, c0:c0 + MXU_COLS] = jnp.dot(k_ext, qext_ref[g, :, c0:c0 + MXU_COLS],
                                                        preferred_element_type=F32)

    def block_gates(src_ref):
        q_ts = [src_ref[:, head_cols(g)].astype(F32).T.astype(BF16) for g in heads]
        gates = [jnp.dot(kmean_ref[g].astype(BF16), q_ts[g], preferred_element_type=F32) for g in heads]
        return q_ts, gates

    def select_blocks(g, q_t, gate, qt):
        blk_id = lax.broadcasted_iota(jnp.int32, (n_blocks, Q_TILE), 0)
        q_blk = qt * Q_TILE_BLOCKS + (lax.broadcasted_iota(jnp.int32, (n_blocks, Q_TILE), 1) >> MOBA_BLOCK_LOG2)
        is_past = blk_id < q_blk
        past_f = jnp.where(is_past, 1.0, 0.0)
        gt = jnp.where(is_past, gate, -jnp.inf)
        sel = jnp.where(blk_id == q_blk, 1.0, 0.0)
        for _ in range(MOBA_TOPK):
            mx = jnp.max(gt, axis=0, keepdims=True)
            first = jnp.min(jnp.where(gt == mx, blk_id, n_blocks), axis=0, keepdims=True)
            pick = blk_id == first
            sel = jnp.where(pick, jnp.maximum(sel, past_f), sel)
            gt = jnp.where(pick, -jnp.inf, gt)
        qext_ref[g, 0:HEAD_DIM, :] = q_t
        qext_ref[g, HEAD_DIM:HEAD_DIM + n_blocks, :] = jnp.where(sel > 0.5, 0.0, MASK_NEG).astype(BF16)

    @pl.when(i == 0)
    def _prepare_keys_values():
        qext_ref[...] = jnp.zeros(qext_ref.shape, BF16)
        qext_ref[:, HEAD_DIM + n_blocks:HEAD_DIM + n_blocks + 16, :] = jnp.full(
            (HEADS_PER_STEP, 16, Q_TILE), MASK_NEG, BF16)
        lane = lax.broadcasted_iota(jnp.int32, (MOBA_BLOCK, HEAD_DIM), 1)
        pad_lane = lax.broadcasted_iota(jnp.int32, (KEY_CHUNK, HEAD_DIM), 1)
        onehot_ref[pad_chunk] = jnp.where(pad_lane == n_blocks, 1.0, 0.0).astype(BF16)
        vt_ref[:, pad_chunk] = jnp.zeros((HEADS_PER_STEP, VT_ROWS, KEY_CHUNK), BF16)
        vt_ref[:, 0:n_chunks, HEAD_DIM:VT_ROWS, :] = jnp.ones(
            (HEADS_PER_STEP, n_chunks, VT_ROWS - HEAD_DIM, KEY_CHUNK), BF16)
        p_even_ref[...] = jnp.zeros(p_even_ref.shape, BF16)
        key_in_chunk = lax.broadcasted_iota(jnp.int32, (KEY_CHUNK, Q_TILE), 0)
        query_in_tile = lax.broadcasted_iota(jnp.int32, (KEY_CHUNK, Q_TILE), 1)
        causal_ref[...] = jnp.where(key_in_chunk > query_in_tile, MASK_NEG, 0.0).astype(F32)

        def chunk_body(c, carry):
            for s in range(KEY_CHUNK_BLOCKS):
                blk = c * KEY_CHUNK_BLOCKS + s
                rows = pl.ds(pl.multiple_of(blk * MOBA_BLOCK, MOBA_BLOCK), MOBA_BLOCK)
                onehot_ref[c, pl.ds(s * MOBA_BLOCK, MOBA_BLOCK), :] = jnp.where(lane == blk, 1.0, 0.0).astype(BF16)
                for g in heads:
                    kb = k_ref[rows, head_cols(g)].astype(F32)
                    kmean_ref[g, pl.ds(blk, 1), :] = jnp.mean(kb, axis=0, keepdims=True)
            return carry

        lax.fori_loop(0, n_chunks, chunk_body, 0)
        q_ts, gates = block_gates(q_first_ref)
        for g in heads:
            select_blocks(g, q_ts[g], gates[g], i)
        scores_stage(s_even_ref, 0, lead_of(i))

    c_own = i
    lead = lead_of(i)

    def softmax_stage(g, s_ref, p_ref, a_ref, causal=False):
        def scores_rows(r0):
            s_t = s_ref[g, r0:r0 + SOFTMAX_ROWS, :]
            if causal:
                s_t = s_t + causal_ref[r0:r0 + SOFTMAX_ROWS, :]
            return s_t

        row_starts = range(0, KEY_CHUNK, SOFTMAX_ROWS)
        running = scores_rows(0)
        for r0 in row_starts[1:]:
            running = jnp.maximum(running, scores_rows(r0))
        m_old = m_ref[g]
        m_new = jnp.maximum(m_old, jnp.max(running, axis=0, keepdims=True))
        m_ref[g] = m_new
        a_ref[g] = jnp.exp2(m_old - m_new)
        for r0 in row_starts:
            p_ref[g, r0:r0 + SOFTMAX_ROWS, :] = jnp.exp2(scores_rows(r0) - m_new).astype(BF16)

    def value_stage(g, pos, p_ref, a_ref):
        acc_ref[g] = a_ref[g] * acc_ref[g] + jnp.dot(vt_ref[g, chunk_at(pos, lead)], p_ref[g],
                                                     preferred_element_type=F32)

    def pair_body(u, carry):
        pos = 2 * u + 1
        for g in heads:
            value_stage(g, pos - 2, p_even_ref, a_even_ref)
        scores_stage(s_odd_ref, pos, lead)
        for g in heads:
            softmax_stage(g, s_even_ref, p_odd_ref, a_odd_ref)
        scores_stage(s_even_ref, pos + 1, lead)
        for g in heads:
            softmax_stage(g, s_odd_ref, p_even_ref, a_even_ref)
        for g in heads:
            value_stage(g, pos - 1, p_odd_ref, a_odd_ref)
        return carry

    acc_ref[...] = jnp.zeros(acc_ref.shape, F32)
    m_ref[...] = jnp.full(m_ref.shape, MASK_NEG, F32)
    a_even_ref[...] = jnp.ones(a_even_ref.shape, F32)
    last = c_own + lead
    lax.fori_loop(0, last // 2, pair_body, 0)

    for g in heads:
        for s in range(KEY_CHUNK_BLOCKS):
            vb = v_ref[s * MOBA_BLOCK:(s + 1) * MOBA_BLOCK, head_cols(g)].astype(F32)
            vt_ref[g, c_own, 0:HEAD_DIM, s * MOBA_BLOCK:(s + 1) * MOBA_BLOCK] = vb.T.astype(BF16)
    q_ts, gates = block_gates(q_next_ref)
    for g in heads:
        select_blocks(g, q_ts[g], gates[g], i + 1)
    for g in heads:
        softmax_stage(g, s_even_ref, p_odd_ref, a_odd_ref, causal=True)
        value_stage(g, last - 1, p_even_ref, a_even_ref)
        value_stage(g, last, p_odd_ref, a_odd_ref)
        scores_stage(s_even_ref, 0, lead_of(i + 1), which_heads=(g,))
        denom = acc_ref[g, HEAD_DIM:HEAD_DIM + 1, :]
        out = (acc_ref[g, 0:HEAD_DIM, :] * (1.0 / denom)).T * sg_ref[:, head_cols(g)].astype(F32)
        o_ref[:, head_cols(g)] = out.astype(o_ref.dtype)


def _attn_branch(z, batch, seq):
    assert Q_TILE == KEY_CHUNK
    nb = seq // MOBA_BLOCK
    n_chunks = nb // KEY_CHUNK_BLOCKS
    nt = seq // Q_TILE
    gw = HEADS_PER_STEP * HEAD_DIM
    qblk, kblk, vblk, gblk = (c // gw for c in (COL_Q, COL_K, COL_V, COL_GA))
    return pl.pallas_call(
        functools.partial(_attn_kernel, n_blocks=nb),
        grid=(batch, N_HEADS // HEADS_PER_STEP, nt),
        in_specs=[pl.BlockSpec((Q_TILE, gw), lambda b, h, i: (b * nt, qblk + h)),
                  pl.BlockSpec((Q_TILE, gw), lambda b, h, i: (b * nt + jnp.minimum(i + 1, nt - 1), qblk + h)),
                  pl.BlockSpec((seq, gw), lambda b, h, i: (b, kblk + h)),
                  pl.BlockSpec((KEY_CHUNK, gw), lambda b, h, i: (b * nt + i, vblk + h)),
                  pl.BlockSpec((Q_TILE, gw), lambda b, h, i: (b * nt + i, gblk + h))],
        out_specs=pl.BlockSpec((Q_TILE, gw), lambda b, h, i: (b * nt + i, h)),
        out_shape=jax.ShapeDtypeStruct((batch * seq, ATTN_WIDTH), BF16),
        scratch_shapes=[pltpu.VMEM((n_chunks + 1, KEY_CHUNK, HEAD_DIM), BF16),
                        pltpu.VMEM((HEADS_PER_STEP, n_chunks + 1, VT_ROWS, KEY_CHUNK), BF16),
                        pltpu.VMEM((HEADS_PER_STEP, nb, HEAD_DIM), F32),
                        pltpu.VMEM((HEADS_PER_STEP, 2 * HEAD_DIM, Q_TILE), BF16),
                        pltpu.VMEM((HEADS_PER_STEP, KEY_CHUNK, Q_TILE), F32),
                        pltpu.VMEM((HEADS_PER_STEP, KEY_CHUNK, Q_TILE), F32),
                        pltpu.VMEM((HEADS_PER_STEP, KEY_CHUNK, Q_TILE), BF16),
                        pltpu.VMEM((HEADS_PER_STEP, KEY_CHUNK, Q_TILE), BF16),
                        pltpu.VMEM((HEADS_PER_STEP, VT_ROWS, Q_TILE), F32),
                        pltpu.VMEM((HEADS_PER_STEP, 1, Q_TILE), F32),
                        pltpu.VMEM((HEADS_PER_STEP, 1, Q_TILE), F32),
                        pltpu.VMEM((HEADS_PER_STEP, 1, Q_TILE), F32),
                        pltpu.VMEM((KEY_CHUNK, Q_TILE), F32)],
        compiler_params=_params("arbitrary", "arbitrary", "arbitrary"),
        name="moba_attention",
    )(z, z, z, z, z)


def _merge_kernel(yp_ref, ya_ref, wp_ref, wa_ref, mp_ref, ma_ref, o_ref):
    def matmul(rows, cols):
        return (jnp.dot(yp_ref[rows, :], wp_ref[:, cols], preferred_element_type=F32),
                jnp.dot(ya_ref[rows, :], wa_ref[:, cols], preferred_element_type=F32))

    def finish(rows, cols, acc):
        pool, attn = acc
        merged = mp_ref[rows, cols].astype(F32) * pool + ma_ref[rows, cols].astype(F32) * attn
        o_ref[rows, cols] = merged.astype(o_ref.dtype)

    _shadowed_subtiles(o_ref.shape, matmul, finish)


def _merge(y_pool, y_attn, w_pool_out, w_attn_out, z, tm=1024, tn=1024):
    t = y_pool.shape[0]
    mpblk, mablk = COL_MP // tn, COL_MA // tn
    return pl.pallas_call(
        _merge_kernel,
        grid=(t // tm, D_MODEL // tn),
        in_specs=[pl.BlockSpec((tm, POOL_WIDTH), lambda i, j: (i, 0)),
                  pl.BlockSpec((tm, ATTN_WIDTH), lambda i, j: (i, 0)),
                  pl.BlockSpec((POOL_WIDTH, tn), lambda i, j: (0, j)),
                  pl.BlockSpec((ATTN_WIDTH, tn), lambda i, j: (0, j)),
                  pl.BlockSpec((tm, tn), lambda i, j: (i, mpblk + j)),
                  pl.BlockSpec((tm, tn), lambda i, j: (i, mablk + j))],
        out_specs=pl.BlockSpec((tm, tn), lambda i, j: (i, j)),
        out_shape=jax.ShapeDtypeStruct((t, D_MODEL), BF16),
        compiler_params=_params("arbitrary", "arbitrary"),
        name="gated_merge",
    )(y_pool, y_attn, w_pool_out, w_attn_out, z, z)


def _matmul_kernel(a_ref, b_ref, o_ref):
    _subtiled_matmul(a_ref, b_ref, o_ref, lambda acc: acc)


def _out_proj(a, w, tm=1024, tn=1024):
    t, k = a.shape
    n = w.shape[1]
    return pl.pallas_call(
        _matmul_kernel,
        grid=(t // tm, n // tn),
        in_specs=[pl.BlockSpec((tm, k), lambda i, j: (i, 0)),
                  pl.BlockSpec((k, tn), lambda i, j: (0, j))],
        out_specs=pl.BlockSpec((tm, tn), lambda i, j: (i, j)),
        out_shape=jax.ShapeDtypeStruct((t, n), BF16),
        compiler_params=_params("arbitrary", "arbitrary"),
        name="out_proj",
    )(a, w)


def _ple_kernel(x_ref, t_ref, g_ref, wg_ref, p_ref, wp_ref, o_ref, x1b_ref, rstd_ref, *, tn):
    j = pl.program_id(1)

    @pl.when(j == 0)
    def _():
        def row_block(r, carry):
            rows = pl.ds(pl.multiple_of(r * NORM_ROWS, NORM_ROWS), NORM_ROWS)
            t = t_ref[rows, :].astype(F32)
            rstd = lax.rsqrt(jnp.mean(t * t, axis=-1, keepdims=True) + RMS_EPS)
            rstd_ref[rows, :] = rstd
            x1b_ref[rows, :] = (x_ref[rows, :] + t * rstd * g_ref[...]).astype(BF16)
            return carry

        lax.fori_loop(0, x_ref.shape[0] // NORM_ROWS, row_block, 0, unroll=4)

    def matmul(rows, cols):
        return (jnp.dot(x1b_ref[rows, :], wg_ref[:, cols], preferred_element_type=F32),
                jnp.dot(p_ref[rows, :].astype(BF16), wp_ref[:, cols], preferred_element_type=F32))

    def finish(rows, cols, acc):
        gate, emb = acc
        xcols = pl.ds(pl.multiple_of(j * tn + cols.start, MXU_COLS), MXU_COLS)
        x1 = x_ref[rows, xcols] + t_ref[rows, xcols].astype(F32) * rstd_ref[rows, :] * g_ref[:, xcols]
        o_ref[rows, cols] = x1 + _sigmoid(gate) * emb

    _shadowed_subtiles(o_ref.shape, matmul, finish)


def _ple(x2d, t_proj, gain, w_gate, p2d, w_proj, tm=512, tn=1024):
    t, d = x2d.shape
    return pl.pallas_call(
        functools.partial(_ple_kernel, tn=tn),
        grid=(t // tm, d // tn),
        in_specs=[pl.BlockSpec((tm, d), lambda i, j: (i, 0)),
                  pl.BlockSpec((tm, d), lambda i, j: (i, 0)),
                  pl.BlockSpec((1, d), lambda i, j: (0, 0)),
                  pl.BlockSpec((d, tn), lambda i, j: (0, j)),
                  pl.BlockSpec((tm, PLE_DIM), lambda i, j: (i, 0)),
                  pl.BlockSpec((PLE_DIM, tn), lambda i, j: (0, j))],
        out_specs=pl.BlockSpec((tm, tn), lambda i, j: (i, j)),
        out_shape=jax.ShapeDtypeStruct((t, d), F32),
        scratch_shapes=[pltpu.VMEM((tm, d), BF16),
                        pltpu.VMEM((tm, 1), F32)],
        compiler_params=_params("arbitrary", "arbitrary"),
        name="ple_residual",
    )(x2d, t_proj, gain, w_gate, p2d, w_proj)


def kernel(x, p, norm_pre, w_in, pool_group_w, pool_scale, w_pool_out, w_attn_out, w_out, norm_post, w_ple_proj, w_ple_gate):
    batch, seq, d = x.shape
    depth = w_in.shape[0]
    assert d == D_MODEL and seq % KEY_CHUNK == 0 and w_in.shape[2] == IN_WIDTH
    x2d = x.reshape(batch * seq, d)
    for layer in range(depth):
        h = _rmsnorm(x2d, norm_pre[layer].reshape(1, d))
        z, (w_pool_b, w_attn_b, w_out_b, w_gate_b) = _in_proj(
            h, w_in[layer].astype(BF16),
            (w_pool_out[layer], w_attn_out[layer], w_out[layer], w_ple_gate[layer]))
        y_pool = _pool_branch(z, pool_group_w[layer].astype(BF16), pool_scale[layer].reshape(1, POOL_WIDTH), batch, seq)
        y_attn = _attn_branch(z, batch, seq)
        merged = _merge(y_pool, y_attn, w_pool_b, w_attn_b, z)
        t_proj = _out_proj(merged, w_out_b)
        x2d = _ple(x2d, t_proj, norm_post[layer].reshape(1, d), w_gate_b,
                   p[layer].reshape(batch * seq, PLE_DIM), w_ple_proj[layer].astype(BF16))
    return x2d.reshape(batch, seq, d)
```

```python
import functools

import jax
import jax.numpy as jnp
from jax import lax
from jax.experimental import pallas as pl
from jax.experimental.pallas import tpu as pltpu

F32 = jnp.float32
BF16 = jnp.bfloat16

D_MODEL = 4096
PLE_DIM = 256
POOL_WIDTH = D_MODEL // 2
POOL_WINDOWS = (2, 4, 8, 16)
POOL_GROUP_W = POOL_WIDTH // len(POOL_WINDOWS)
assert all(w == 1 << (w.bit_length() - 1) for w in POOL_WINDOWS)
POOL_HALO = 8 * (max(POOL_WINDOWS).bit_length() - 1)
HEAD_DIM = 128
ATTN_WIDTH = D_MODEL // 2
N_HEADS = ATTN_WIDTH // HEAD_DIM
MOBA_BLOCK = 256
MOBA_BLOCK_LOG2 = MOBA_BLOCK.bit_length() - 1
assert MOBA_BLOCK == 1 << MOBA_BLOCK_LOG2
MOBA_TOPK = 3
RMS_EPS = 1e-6
IN_WIDTH = 2 * POOL_WIDTH + 4 * ATTN_WIDTH + 2 * D_MODEL

COL_U = 0
COL_GP = COL_U + POOL_WIDTH
COL_Q = COL_GP + POOL_WIDTH
COL_K = COL_Q + ATTN_WIDTH
COL_V = COL_K + ATTN_WIDTH
COL_GA = COL_V + ATTN_WIDTH
COL_MP = COL_GA + ATTN_WIDTH
COL_MA = COL_MP + D_MODEL

KEY_CHUNK_BLOCKS = 2
KEY_CHUNK = KEY_CHUNK_BLOCKS * MOBA_BLOCK
HEADS_PER_STEP = 4
Q_TILE_BLOCKS = 2
Q_TILE = Q_TILE_BLOCKS * MOBA_BLOCK
SOFTMAX_ROWS = 16
VT_ROWS = HEAD_DIM + 16
LOG2_E = 1.4426950408889634
Q_SCALE = HEAD_DIM ** -0.5 * LOG2_E
MASK_NEG = -1e30

V7X_VMEM_LIMIT_BYTES = 56 * 1024 * 1024
LANES = 128
CAST_ROWS = 16
NORM_ROWS = 16
MXU_COLS = 256
MXU_ROWS_PER_DOT = 512


def _params(*semantics):
    return pltpu.CompilerParams(dimension_semantics=semantics,
                                vmem_limit_bytes=V7X_VMEM_LIMIT_BYTES)


def _sigmoid(v):
    return 0.5 * jnp.tanh(0.5 * v) + 0.5


def _silu(v):
    h = 0.5 * v
    return h + h * jnp.tanh(h)


def _rmsnorm_kernel(x_ref, g_ref, o_ref):
    x = x_ref[...]
    ms = jnp.mean(x * x, axis=-1, keepdims=True)
    o_ref[...] = (x * lax.rsqrt(ms + RMS_EPS) * g_ref[...]).astype(o_ref.dtype)


def _rmsnorm(x2d, gain, tm=512):
    t, d = x2d.shape
    return pl.pallas_call(
        _rmsnorm_kernel,
        grid=(t // tm,),
        in_specs=[pl.BlockSpec((tm, d), lambda i: (i, 0)),
                  pl.BlockSpec((1, d), lambda i: (0, 0))],
        out_specs=pl.BlockSpec((tm, d), lambda i: (i, 0)),
        out_shape=jax.ShapeDtypeStruct((t, d), BF16),
        compiler_params=_params("arbitrary"),
        name="rmsnorm_pre",
    )(x2d, gain)


def _shadowed_subtiles(tile_shape, matmul, finish):
    tm, tn = tile_shape
    pending = None
    for c0 in range(0, tn, MXU_COLS):
        for r0 in range(0, tm, min(tm, MXU_ROWS_PER_DOT)):
            rows, cols = slice(r0, r0 + min(tm, MXU_ROWS_PER_DOT)), slice(c0, c0 + MXU_COLS)
            acc = matmul(rows, cols)
            if pending is not None:
                finish(*pending)
            pending = (rows, cols, acc)
    finish(*pending)


def _subtiled_matmul(a_ref, b_ref, o_ref, epilogue):
    def matmul(rows, cols):
        return jnp.dot(a_ref[rows, :], b_ref[:, cols], preferred_element_type=F32)

    def finish(rows, cols, acc):
        o_ref[rows, cols] = epilogue(acc).astype(o_ref.dtype)

    _shadowed_subtiles(o_ref.shape, matmul, finish)


def _win_kernel(h_ref, w_ref, *refs, tn, n_side):
    side_in, o_ref, side_out = refs[:n_side], refs[n_side], refs[n_side + 1:]
    for src_ref, dst_ref in zip(side_in, side_out):
        dst_ref[...] = src_ref[...].astype(dst_ref.dtype)
    col0 = pl.program_id(1) * tn
    is_silu = ((col0 >= COL_GP) & (col0 < COL_Q)) | ((col0 >= COL_GA) & (col0 < COL_MP))
    is_q = (col0 >= COL_Q) & (col0 < COL_K)
    is_sig = col0 >= COL_MP
    is_plain = jnp.logical_not(is_silu | is_q | is_sig)
    kinds = ((is_plain, lambda acc: acc),
             (is_q, lambda acc: acc * Q_SCALE),
             (is_silu, _silu),
             (is_sig, _sigmoid))
    for pred, epilogue in kinds:
        pl.when(pred)(functools.partial(_subtiled_matmul, h_ref, w_ref, o_ref, epilogue))


def _in_proj(h, w_in, side_weights, tm=1024, tn=1024):
    t, k = h.shape
    n = w_in.shape[1]
    n_j = n // tn
    steps = (t // tm) * n_j
    side_specs, side_shapes = [], []
    for w in side_weights:
        rows, cols = w.shape
        assert rows % CAST_ROWS == 0 and rows // CAST_ROWS <= steps
        last = rows // CAST_ROWS - 1
        spec = pl.BlockSpec((CAST_ROWS, cols), lambda i, j, last=last: (jnp.minimum(i * n_j + j, last), 0))
        side_specs.append(spec)
        side_shapes.append(jax.ShapeDtypeStruct(w.shape, BF16))
    outs = pl.pallas_call(
        functools.partial(_win_kernel, tn=tn, n_side=len(side_weights)),
        grid=(t // tm, n_j),
        in_specs=[pl.BlockSpec((tm, k), lambda i, j: (i, 0)),
                  pl.BlockSpec((k, tn), lambda i, j: (0, j))] + side_specs,
        out_specs=[pl.BlockSpec((tm, tn), lambda i, j: (i, j))] + side_specs,
        out_shape=[jax.ShapeDtypeStruct((t, n), BF16)] + side_shapes,
        compiler_params=_params("arbitrary", "arbitrary"),
        name="in_proj",
    )(h, w_in, *side_weights)
    return outs[0], outs[1:]


def _pool_kernel(u_ref, sg_ref, w_ref, sc_ref, o_ref, hist_ref, lvl_a_ref, lvl_b_ref, *, tm):
    ti = pl.program_id(1)
    total = POOL_HALO + tm

    @pl.when(ti == 0)
    def _():
        hist_ref[0:POOL_HALO, :] = jnp.zeros((POOL_HALO, POOL_WIDTH), F32)

    @pl.when(ti > 0)
    def _():
        hist_ref[0:POOL_HALO, :] = hist_ref[tm:tm + POOL_HALO, :]

    hist_ref[POOL_HALO:POOL_HALO + tm, :] = u_ref[...].astype(F32)
    pos = ti * tm + lax.broadcasted_iota(jnp.int32, (tm, 1), 0)
    for g, w in enumerate(POOL_WINDOWS):
        c0 = g * POOL_GROUP_W
        cols = pl.ds(c0, POOL_GROUP_W)
        u = hist_ref[pl.ds(POOL_HALO, tm), cols]
        src_ref, src_cols = hist_ref, cols
        n_levels = w.bit_length() - 1
        for k in range(n_levels):
            lo = 8 * (k + 1)
            level = (src_ref[pl.ds(lo, total - lo), src_cols]
                     + src_ref[pl.ds(lo - (1 << k), total - lo), src_cols])
            if k + 1 < n_levels:
                dst_ref = lvl_a_ref if k % 2 == 0 else lvl_b_ref
                dst_ref[pl.ds(lo, total - lo), :] = level
                src_ref, src_cols = dst_ref, slice(None)
        acc = level[POOL_HALO - lo:, :]
        count = jnp.minimum(pos + 1, w).astype(F32)
        diff = acc / count - u
        mixed = jnp.dot(diff.astype(BF16), w_ref[g], preferred_element_type=F32)
        y = mixed * sc_ref[:, cols] * sg_ref[:, cols].astype(F32)
        o_ref[:, cols] = y.astype(o_ref.dtype)


def _pool_branch(z, group_w, scale, batch, seq, tm=512):
    nt = seq // tm
    ublk = COL_U // POOL_WIDTH
    gblk = COL_GP // POOL_WIDTH
    return pl.pallas_call(
        functools.partial(_pool_kernel, tm=tm),
        grid=(batch, nt),
        in_specs=[pl.BlockSpec((tm, POOL_WIDTH), lambda b, i: (b * nt + i, ublk)),
                  pl.BlockSpec((tm, POOL_WIDTH), lambda b, i: (b * nt + i, gblk)),
                  pl.BlockSpec(group_w.shape, lambda b, i: (0, 0, 0)),
                  pl.BlockSpec((1, POOL_WIDTH), lambda b, i: (0, 0))],
        out_specs=pl.BlockSpec((tm, POOL_WIDTH), lambda b, i: (b * nt + i, 0)),
        out_shape=jax.ShapeDtypeStruct((batch * seq, POOL_WIDTH), BF16),
        scratch_shapes=[pltpu.VMEM((POOL_HALO + tm, POOL_WIDTH), F32),
                        pltpu.VMEM((POOL_HALO + tm, POOL_GROUP_W), F32),
                        pltpu.VMEM((POOL_HALO + tm, POOL_GROUP_W), F32)],
        compiler_params=_params("arbitrary", "arbitrary"),
        name="pool_branch",
    )(z, z, group_w, scale)


def _attn_kernel(q_first_ref, q_next_ref, k_ref, v_ref, sg_ref, o_ref, onehot_ref, vt_ref, kmean_ref, qext_ref,
                 s_even_ref, s_odd_ref, p_even_ref, p_odd_ref, acc_ref, m_ref, a_even_ref, a_odd_ref, causal_ref, *, n_blocks):
    i = pl.program_id(2)
    n_chunks = n_blocks // KEY_CHUNK_BLOCKS
    pad_chunk = n_chunks
    heads = range(HEADS_PER_STEP)

    def head_cols(g):
        return pl.ds(g * HEAD_DIM, HEAD_DIM)

    def chunk_rows(c):
        return pl.ds(pl.multiple_of(c * KEY_CHUNK, KEY_CHUNK), KEY_CHUNK)

    def lead_of(qt):
        return qt % 2

    def chunk_at(pos, lead):
        c = pos - lead
        return jnp.where(c < 0, pad_chunk, c)

    def scores_stage(s_ref, pos, lead, which_heads=heads):
        c = chunk_at(pos, lead)
        rows = chunk_rows(jnp.minimum(c, n_chunks - 1))
        for g in which_heads:
            k_ext = jnp.concatenate([k_ref[rows, head_cols(g)], onehot_ref[c]], axis=1)
            for c0 in range(0, Q_TILE, MXU_COLS):
                s_ref[g, :, c0:c0 + MXU_COLS] = jnp.dot(k_ext, qext_ref[g, :, c0:c0 + MXU_COLS],
                                                        preferred_element_type=F32)

    def block_gates(src_ref):
        q_ts = [src_ref[:, head_cols(g)].astype(F32).T.astype(BF16) for g in heads]
        gates = [jnp.dot(kmean_ref[g].astype(BF16), q_ts[g], preferred_element_type=F32) for g in heads]
        return q_ts, gates

    def select_blocks(g, q_t, gate, qt):
        blk_id = lax.broadcasted_iota(jnp.int32, (n_blocks, Q_TILE), 0)
        q_blk = qt * Q_TILE_BLOCKS + (lax.broadcasted_iota(jnp.int32, (n_blocks, Q_TILE), 1) >> MOBA_BLOCK_LOG2)
        is_past = blk_id < q_blk
        past_f = jnp.where(is_past, 1.0, 0.0)
        gt = jnp.where(is_past, gate, -jnp.inf)
        sel = jnp.where(blk_id == q_blk, 1.0, 0.0)
        for _ in range(MOBA_TOPK):
            mx = jnp.max(gt, axis=0, keepdims=True)
            first = jnp.min(jnp.where(gt == mx, blk_id, n_blocks), axis=0, keepdims=True)
            pick = blk_id == first
            sel = jnp.where(pick, jnp.maximum(sel, past_f), sel)
            gt = jnp.where(pick, -jnp.inf, gt)
        qext_ref[g, 0:HEAD_DIM, :] = q_t
        qext_ref[g, HEAD_DIM:HEAD_DIM + n_blocks, :] = jnp.where(sel > 0.5, 0.0, MASK_NEG).astype(BF16)

    @pl.when(i == 0)
    def _prepare_keys_values():
        qext_ref[...] = jnp.zeros(qext_ref.shape, BF16)
        qext_ref[:, HEAD_DIM + n_blocks:HEAD_DIM + n_blocks + 16, :] = jnp.full(
            (HEADS_PER_STEP, 16, Q_TILE), MASK_NEG, BF16)
        lane = lax.broadcasted_iota(jnp.int32, (MOBA_BLOCK, HEAD_DIM), 1)
        pad_lane = lax.broadcasted_iota(jnp.int32, (KEY_CHUNK, HEAD_DIM), 1)
        onehot_ref[pad_chunk] = jnp.where(pad_lane == n_blocks, 1.0, 0.0).astype(BF16)
        vt_ref[:, pad_chunk] = jnp.zeros((HEADS_PER_STEP, VT_ROWS, KEY_CHUNK), BF16)
        vt_ref[:, 0:n_chunks, HEAD_DIM:VT_ROWS, :] = jnp.ones(
            (HEADS_PER_STEP, n_chunks, VT_ROWS - HEAD_DIM, KEY_CHUNK), BF16)
        p_even_ref[...] = jnp.zeros(p_even_ref.shape, BF16)
        key_in_chunk = lax.broadcasted_iota(jnp.int32, (KEY_CHUNK, Q_TILE), 0)
        query_in_tile = lax.broadcasted_iota(jnp.int32, (KEY_CHUNK, Q_TILE), 1)
        causal_ref[...] = jnp.where(key_in_chunk > query_in_tile, MASK_NEG, 0.0).astype(F32)

        def chunk_body(c, carry):
            for s in range(KEY_CHUNK_BLOCKS):
                blk = c * KEY_CHUNK_BLOCKS + s
                rows = pl.ds(pl.multiple_of(blk * MOBA_BLOCK, MOBA_BLOCK), MOBA_BLOCK)
                onehot_ref[c, pl.ds(s * MOBA_BLOCK, MOBA_BLOCK), :] = jnp.where(lane == blk, 1.0, 0.0).astype(BF16)
                for g in heads:
                    kb = k_ref[rows, head_cols(g)].astype(F32)
                    kmean_ref[g, pl.ds(blk, 1), :] = jnp.mean(kb, axis=0, keepdims=True)
            return carry

        lax.fori_loop(0, n_chunks, chunk_body, 0)
        q_ts, gates = block_gates(q_first_ref)
        for g in heads:
            select_blocks(g, q_ts[g], gates[g], i)
        scores_stage(s_even_ref, 0, lead_of(i))

    c_own = i
    lead = lead_of(i)

    def softmax_stage(g, s_ref, p_ref, a_ref, causal=False):
        def scores_rows(r0):
            s_t = s_ref[g, r0:r0 + SOFTMAX_ROWS, :]
            if causal:
                s_t = s_t + causal_ref[r0:r0 + SOFTMAX_ROWS, :]
            return s_t

        row_starts = range(0, KEY_CHUNK, SOFTMAX_ROWS)
        running = scores_rows(0)
        for r0 in row_starts[1:]:
            running = jnp.maximum(running, scores_rows(r0))
        m_old = m_ref[g]
        m_new = jnp.maximum(m_old, jnp.max(running, axis=0, keepdims=True))
        m_ref[g] = m_new
        a_ref[g] = jnp.exp2(m_old - m_new)
        for r0 in row_starts:
            p_ref[g, r0:r0 + SOFTMAX_ROWS, :] = jnp.exp2(scores_rows(r0) - m_new).astype(BF16)

    def value_stage(g, pos, p_ref, a_ref):
        acc_ref[g] = a_ref[g] * acc_ref[g] + jnp.dot(vt_ref[g, chunk_at(pos, lead)], p_ref[g],
                                                     preferred_element_type=F32)

    def pair_body(u, carry):
        pos = 2 * u + 1
        for g in heads:
            value_stage(g, pos - 2, p_even_ref, a_even_ref)
        scores_stage(s_odd_ref, pos, lead)
        for g in heads:
            softmax_stage(g, s_even_ref, p_odd_ref, a_odd_ref)
        scores_stage(s_even_ref, pos + 1, lead)
        for g in heads:
            softmax_stage(g, s_odd_ref, p_even_ref, a_even_ref)
        for g in heads:
            value_stage(g, pos - 1, p_odd_ref, a_odd_ref)
        return carry

    acc_ref[...] = jnp.zeros(acc_ref.shape, F32)
    m_ref[...] = jnp.full(m_ref.shape, MASK_NEG, F32)
    a_even_ref[...] = jnp.ones(a_even_ref.shape, F32)
    last = c_own + lead
    lax.fori_loop(0, last // 2, pair_body, 0)

    for g in heads:
        for s in range(KEY_CHUNK_BLOCKS):
            vb = v_ref[s * MOBA_BLOCK:(s + 1) * MOBA_BLOCK, head_cols(g)].astype(F32)
            vt_ref[g, c_own, 0:HEAD_DIM, s * MOBA_BLOCK:(s + 1) * MOBA_BLOCK] = vb.T.astype(BF16)
    q_ts, gates = block_gates(q_next_ref)
    for g in heads:
        select_blocks(g, q_ts[g], gates[g], i + 1)
    for g in heads:
        softmax_stage(g, s_even_ref, p_odd_ref, a_odd_ref, causal=True)
        value_stage(g, last - 1, p_even_ref, a_even_ref)
        value_stage(g, last, p_odd_ref, a_odd_ref)
        scores_stage(s_even_ref, 0, lead_of(i + 1), which_heads=(g,))
        denom = acc_ref[g, HEAD_DIM:HEAD_DIM + 1, :]
        out = (acc_ref[g, 0:HEAD_DIM, :] * (1.0 / denom)).T * sg_ref[:, head_cols(g)].astype(F32)
        o_ref[:, head_cols(g)] = out.astype(o_ref.dtype)


def _attn_branch(z, batch, seq):
    assert Q_TILE == KEY_CHUNK
    nb = seq // MOBA_BLOCK
    n_chunks = nb // KEY_CHUNK_BLOCKS
    nt = seq // Q_TILE
    gw = HEADS_PER_STEP * HEAD_DIM
    qblk, kblk, vblk, gblk = (c // gw for c in (COL_Q, COL_K, COL_V, COL_GA))
    return pl.pallas_call(
        functools.partial(_attn_kernel, n_blocks=nb),
        grid=(batch, N_HEADS // HEADS_PER_STEP, nt),
        in_specs=[pl.BlockSpec((Q_TILE, gw), lambda b, h, i: (b * nt, qblk + h)),
                  pl.BlockSpec((Q_TILE, gw), lambda b, h, i: (b * nt + jnp.minimum(i + 1, nt - 1), qblk + h)),
                  pl.BlockSpec((seq, gw), lambda b, h, i: (b, kblk + h)),
                  pl.BlockSpec((KEY_CHUNK, gw), lambda b, h, i: (b * nt + i, vblk + h)),
                  pl.BlockSpec((Q_TILE, gw), lambda b, h, i: (b * nt + i, gblk + h))],
        out_specs=pl.BlockSpec((Q_TILE, gw), lambda b, h, i: (b * nt + i, h)),
        out_shape=jax.ShapeDtypeStruct((batch * seq, ATTN_WIDTH), BF16),
        scratch_shapes=[pltpu.VMEM((n_chunks + 1, KEY_CHUNK, HEAD_DIM), BF16),
                        pltpu.VMEM((HEADS_PER_STEP, n_chunks + 1, VT_ROWS, KEY_CHUNK), BF16),
                        pltpu.VMEM((HEADS_PER_STEP, nb, HEAD_DIM), F32),
                        pltpu.VMEM((HEADS_PER_STEP, 2 * HEAD_DIM, Q_TILE), BF16),
                        pltpu.VMEM((HEADS_PER_STEP, KEY_CHUNK, Q_TILE), F32),
                        pltpu.VMEM((HEADS_PER_STEP, KEY_CHUNK, Q_TILE), F32),
                        pltpu.VMEM((HEADS_PER_STEP, KEY_CHUNK, Q_TILE), BF16),
                        pltpu.VMEM((HEADS_PER_STEP, KEY_CHUNK, Q_TILE), BF16),
                        pltpu.VMEM((HEADS_PER_STEP, VT_ROWS, Q_TILE), F32),
                        pltpu.VMEM((HEADS_PER_STEP, 1, Q_TILE), F32),
                        pltpu.VMEM((HEADS_PER_STEP, 1, Q_TILE), F32),
                        pltpu.VMEM((HEADS_PER_STEP, 1, Q_TILE), F32),
                        pltpu.VMEM((KEY_CHUNK, Q_TILE), F32)],
        compiler_params=_params("arbitrary", "arbitrary", "arbitrary"),
        name="moba_attention",
    )(z, z, z, z, z)


def _merge_kernel(yp_ref, ya_ref, wp_ref, wa_ref, mp_ref, ma_ref, o_ref):
    def matmul(rows, cols):
        return (jnp.dot(yp_ref[rows, :], wp_ref[:, cols], preferred_element_type=F32),
                jnp.dot(ya_ref[rows, :], wa_ref[:, cols], preferred_element_type=F32))

    def finish(rows, cols, acc):
        pool, attn = acc
        merged = mp_ref[rows, cols].astype(F32) * pool + ma_ref[rows, cols].astype(F32) * attn
        o_ref[rows, cols] = merged.astype(o_ref.dtype)

    _shadowed_subtiles(o_ref.shape, matmul, finish)


def _merge(y_pool, y_attn, w_pool_out, w_attn_out, z, tm=1024, tn=1024):
    t = y_pool.shape[0]
    mpblk, mablk = COL_MP // tn, COL_MA // tn
    return pl.pallas_call(
        _merge_kernel,
        grid=(t // tm, D_MODEL // tn),
        in_specs=[pl.BlockSpec((tm, POOL_WIDTH), lambda i, j: (i, 0)),
                  pl.BlockSpec((tm, ATTN_WIDTH), lambda i, j: (i, 0)),
                  pl.BlockSpec((POOL_WIDTH, tn), lambda i, j: (0, j)),
                  pl.BlockSpec((ATTN_WIDTH, tn), lambda i, j: (0, j)),
                  pl.BlockSpec((tm, tn), lambda i, j: (i, mpblk + j)),
                  pl.BlockSpec((tm, tn), lambda i, j: (i, mablk + j))],
        out_specs=pl.BlockSpec((tm, tn), lambda i, j: (i, j)),
        out_shape=jax.ShapeDtypeStruct((t, D_MODEL), BF16),
        compiler_params=_params("arbitrary", "arbitrary"),
        name="gated_merge",
    )(y_pool, y_attn, w_pool_out, w_attn_out, z, z)


def _matmul_kernel(a_ref, b_ref, o_ref):
    _subtiled_matmul(a_ref, b_ref, o_ref, lambda acc: acc)


def _out_proj(a, w, tm=1024, tn=1024):
    t, k = a.shape
    n = w.shape[1]
    return pl.pallas_call(
        _matmul_kernel,
        grid=(t // tm, n // tn),
        in_specs=[pl.BlockSpec((tm, k), lambda i, j: (i, 0)),
                  pl.BlockSpec((k, tn), lambda i, j: (0, j))],
        out_specs=pl.BlockSpec((tm, tn), lambda i, j: (i, j)),
        out_shape=jax.ShapeDtypeStruct((t, n), BF16),
        compiler_params=_params("arbitrary", "arbitrary"),
        name="out_proj",
    )(a, w)


def _ple_kernel(x_ref, t_ref, xt_ref, tt_ref, g_ref, wg_ref, p_ref, wp_ref, o_ref, x1b_ref, rstd_ref, *, tn):
    j = pl.program_id(1)

    @pl.when(j == 0)
    def _():
        def row_block(r, carry):
            rows = pl.ds(pl.multiple_of(r * NORM_ROWS, NORM_ROWS), NORM_ROWS)
            t = t_ref[rows, :].astype(F32)
            rstd = lax.rsqrt(jnp.mean(t * t, axis=-1, keepdims=True) + RMS_EPS)
            rstd_ref[rows, :] = rstd
            x1b_ref[rows, :] = (x_ref[rows, :] + t * rstd * g_ref[...]).astype(BF16)
            return carry

        lax.fori_loop(0, x_ref.shape[0] // NORM_ROWS, row_block, 0, unroll=4)

    def matmul(rows, cols):
        return (jnp.dot(x1b_ref[rows, :], wg_ref[:, cols], preferred_element_type=F32),
                jnp.dot(p_ref[rows, :].astype(BF16), wp_ref[:, cols], preferred_element_type=F32))

    def finish(rows, cols, acc):
        gate, emb = acc
        gcols = pl.ds(pl.multiple_of(j * tn + cols.start, MXU_COLS), MXU_COLS)
        x1 = xt_ref[rows, cols] + tt_ref[rows, cols].astype(F32) * rstd_ref[rows, :] * g_ref[:, gcols]
        o_ref[rows, cols] = x1 + _sigmoid(gate) * emb

    _shadowed_subtiles(o_ref.shape, matmul, finish)


def _ple(x2d, t_proj, gain, w_gate, p2d, w_proj, tm=512, tn=512):
    t, d = x2d.shape
    n_i = t // tm

    def rows_ahead(i, j):
        return (jnp.minimum(i + jnp.minimum(j, 1), n_i - 1), 0)

    return pl.pallas_call(
        functools.partial(_ple_kernel, tn=tn),
        grid=(n_i, d // tn),
        in_specs=[pl.BlockSpec((tm, d), rows_ahead),
                  pl.BlockSpec((tm, d), rows_ahead),
                  pl.BlockSpec((tm, tn), lambda i, j: (i, j)),
                  pl.BlockSpec((tm, tn), lambda i, j: (i, j)),
                  pl.BlockSpec((1, d), lambda i, j: (0, 0)),
                  pl.BlockSpec((d, tn), lambda i, j: (0, j)),
                  pl.BlockSpec((tm, PLE_DIM), lambda i, j: (i, 0)),
                  pl.BlockSpec((PLE_DIM, tn), lambda i, j: (0, j))],
        out_specs=pl.BlockSpec((tm, tn), lambda i, j: (i, j)),
        out_shape=jax.ShapeDtypeStruct((t, d), F32),
        scratch_shapes=[pltpu.VMEM((tm, d), BF16),
                        pltpu.VMEM((tm, 1), F32)],
        compiler_params=_params("arbitrary", "arbitrary"),
        name="ple_residual",
    )(x2d, t_proj, x2d, t_proj, gain, w_gate, p2d, w_proj)


def kernel(x, p, norm_pre, w_in, pool_group_w, pool_scale, w_pool_out, w_attn_out, w_out, norm_post, w_ple_proj, w_ple_gate):
    batch, seq, d = x.shape
    depth = w_in.shape[0]
    assert d == D_MODEL and seq % KEY_CHUNK == 0 and w_in.shape[2] == IN_WIDTH
    x2d = x.reshape(batch * seq, d)
    for layer in range(depth):
        h = _rmsnorm(x2d, norm_pre[layer].reshape(1, d))
        z, (w_pool_b, w_attn_b, w_out_b, w_gate_b) = _in_proj(
            h, w_in[layer].astype(BF16),
            (w_pool_out[layer], w_attn_out[layer], w_out[layer], w_ple_gate[layer]))
        y_pool = _pool_branch(z, pool_group_w[layer].astype(BF16), pool_scale[layer].reshape(1, POOL_WIDTH), batch, seq)
        y_attn = _attn_branch(z, batch, seq)
        merged = _merge(y_pool, y_attn, w_pool_b, w_attn_b, z)
        t_proj = _out_proj(merged, w_out_b)
        x2d = _ple(x2d, t_proj, norm_post[layer].reshape(1, d), w_gate_b,
                   p[layer].reshape(batch * seq, PLE_DIM), w_ple_proj[layer].astype(BF16))
    return x2d.reshape(batch, seq, d)
```
